```python
import math
import jax
import jax.numpy as jnp
from jax import lax
import numpy as np

D_MODEL = 1024
BATCH = 1
SEQ = 16384
DEPTH = 4

GRID_W = 64
CTX_LEN = 256
HEAD_DIM = 64
A_HEADS = 4
A_VDIM = 2 * HEAD_DIM
B_HEADS = 8
B_KV_HEADS = 2
C_HEADS = D_MODEL // HEAD_DIM
NA_WIN_H = 8
NA_WIN_W = 16
RPB_H = 2 * NA_WIN_H - 1
RPB_W = 2 * NA_WIN_W - 1
N_GROUPS = 4
EXPERTS_PER_GROUP = 8
N_EXPERTS = N_GROUPS * EXPERTS_PER_GROUP
TOP_K_INNER = 2
D_EXPERT = 512
MOE_BLOCK = 128
Q_BLOCK = 128
ROPE_THETA = 10000.0
N_EVEN = (DEPTH + 1) // 2
N_ODD = DEPTH // 2
DN_ALPHA = (2.0 * DEPTH) ** 0.25
DN_BETA = (8.0 * DEPTH) ** -0.25
A_QK = A_HEADS * 2 * HEAD_DIM
A_V = A_HEADS * A_VDIM
B_Q = B_HEADS * HEAD_DIM
B_KV = B_KV_HEADS * HEAD_DIM
EVEN_IN = 2 * A_QK + A_V + B_Q + 2 * B_KV
EVEN_OUT = A_V + B_Q
C_WIDTH = C_HEADS * HEAD_DIM
ODD_IN = 3 * C_WIDTH

kernel_name = "hybrid_diffattn_gqa_natten_hmoe_deepnorm"


def layer_norm(x, g, b, eps=1e-6):
    xf = x.astype(jnp.float32)
    mu = jnp.mean(xf, axis=-1, keepdims=True)
    var = jnp.mean(jnp.square(xf - mu), axis=-1, keepdims=True)
    return ((xf - mu) * lax.rsqrt(var + eps) * g.astype(jnp.float32) + b.astype(jnp.float32)).astype(x.dtype)


def rms_norm(x, g, eps=1e-6):
    xf = x.astype(jnp.float32)
    ms = jnp.mean(jnp.square(xf), axis=-1, keepdims=True)
    return (xf * lax.rsqrt(ms + eps) * g.astype(jnp.float32)).astype(x.dtype)


def softmax_f32(s):
    return jax.nn.softmax(s.astype(jnp.float32), axis=-1)


def axial_rope_tables(n_tokens):
    t = jnp.arange(n_tokens, dtype=jnp.int32)
    row = (t // GRID_W).astype(jnp.float32)
    col = (t % GRID_W).astype(jnp.float32)
    half = HEAD_DIM // 2
    inv = ROPE_THETA ** (-jnp.arange(0, half, 2, dtype=jnp.float32) / half)
    ang_r = row[:, None] * inv[None, :]
    ang_c = col[:, None] * inv[None, :]
    ang = jnp.concatenate([ang_r, ang_r, ang_c, ang_c], axis=-1)
    return jnp.cos(ang), jnp.sin(ang)


def rotate_axial(x):
    q = HEAD_DIM // 4
    x1, x2, x3, x4 = x[..., :q], x[..., q:2 * q], x[..., 2 * q:3 * q], x[..., 3 * q:]
    return jnp.concatenate([-x2, x1, -x4, x3], axis=-1)


def apply_rope(x, cos, sin):
    shape = (1, x.shape[1]) + (1,) * (x.ndim - 3) + (HEAD_DIM,)
    xf = x.astype(jnp.float32)
    return (xf * cos.reshape(shape) + rotate_axial(xf) * sin.reshape(shape)).astype(x.dtype)


def neighbourhood_tables(n_tokens):
    rows = n_tokens // GRID_W
    kh = min(NA_WIN_H, rows)
    kw = NA_WIN_W
    t = jnp.arange(n_tokens, dtype=jnp.int32)
    r = t // GRID_W
    c = t % GRID_W
    rs = jnp.clip(r - kh // 2, 0, rows - kh)
    cs = jnp.clip(c - kw // 2, 0, GRID_W - kw)
    kr = rs[:, None, None] + jnp.arange(kh, dtype=jnp.int32)[None, :, None]
    kc = cs[:, None, None] + jnp.arange(kw, dtype=jnp.int32)[None, None, :]
    idx = (kr * GRID_W + kc).reshape(n_tokens, kh * kw)
    bidx = ((kr - r[:, None, None] + (NA_WIN_H - 1)) * RPB_W
            + (kc - c[:, None, None] + (NA_WIN_W - 1))).reshape(n_tokens, kh * kw)
    return idx, bidx


def to_blocks(a):
    b, n = a.shape[:2]
    return jnp.swapaxes(a.reshape((b, n // Q_BLOCK, Q_BLOCK) + a.shape[2:]), 0, 1)


def from_blocks(o):
    nb, b, q = o.shape[:3]
    return jnp.swapaxes(o, 0, 1).reshape((b, nb * q) + o.shape[3:])


def diff_attn_core(q, k, v, lam, lam_init, sub_g):
    b, nq = q.shape[:2]
    s = jnp.einsum('bqhmd,bkhmd->bhmqk', q, k) * (HEAD_DIM ** -0.5)
    p = softmax_f32(s)
    a = (p[:, :, 0] - lam * p[:, :, 1]).astype(v.dtype)
    o = jnp.einsum('bhqk,bkhe->bqhe', a, v)
    o = rms_norm(o, sub_g) * (1.0 - lam_init)
    return o.reshape(b, nq, A_V)


def gqa_core(q, k, v):
    b, nq = q.shape[:2]
    q = q.reshape(b, nq, B_KV_HEADS, B_HEADS // B_KV_HEADS, HEAD_DIM)
    s = jnp.einsum('bqgrd,bkgd->bgrqk', q, k) * (HEAD_DIM ** -0.5)
    p = softmax_f32(s).astype(v.dtype)
    o = jnp.einsum('bgrqk,bkgd->bqgrd', p, v)
    return o.reshape(b, nq, B_Q)


def even_mixer(h_lat, h_ctx, w_in, w_out, lam_vec, sub_g, qk_g, lam_init, cos, sin, with_ctx):
    b = h_lat.shape[0]
    splits = [A_QK, 2 * A_QK, 2 * A_QK + A_V, 2 * A_QK + A_V + B_Q, 2 * A_QK + A_V + B_Q + B_KV]

    def project(h):
        s = h.shape[1]
        aq, ak, av, bq, bk, bv = jnp.split(h @ w_in, splits, axis=-1)
        aq = aq.reshape(b, s, A_HEADS, 2, HEAD_DIM)
        ak = ak.reshape(b, s, A_HEADS, 2, HEAD_DIM)
        av = av.reshape(b, s, A_HEADS, A_VDIM)
        bq = rms_norm(bq.reshape(b, s, B_HEADS, HEAD_DIM), qk_g[0])
        bk = rms_norm(bk.reshape(b, s, B_KV_HEADS, HEAD_DIM), qk_g[1])
        bv = bv.reshape(b, s, B_KV_HEADS, HEAD_DIM)
        return aq, ak, av, bq, bk, bv

    aq_l, ak_l, av_l, bq_l, bk_l, bv_l = project(h_lat)
    aq_l = apply_rope(aq_l, cos, sin)
    ak_l = apply_rope(ak_l, cos, sin)
    bq_l = apply_rope(bq_l, cos, sin)
    bk_l = apply_rope(bk_l, cos, sin)
    aq_c, ak_c, av_c, bq_c, bk_c, bv_c = project(h_ctx)
    lamf = lam_vec.astype(jnp.float32)
    lam = jnp.exp(jnp.sum(lamf[0] * lamf[1])) - jnp.exp(jnp.sum(lamf[2] * lamf[3])) + lam_init
    ak_all = jnp.concatenate([ak_c, ak_l], axis=1)
    av_all = jnp.concatenate([av_c, av_l], axis=1)
    bk_all = jnp.concatenate([bk_c, bk_l], axis=1)
    bv_all = jnp.concatenate([bv_c, bv_l], axis=1)

    def block(args):
        aq_b, bq_b = args
        return jnp.concatenate([diff_attn_core(aq_b, ak_all, av_all, lam, lam_init, sub_g),
                                gqa_core(bq_b, bk_all, bv_all)], axis=-1)

    out_l = from_blocks(lax.map(block, (to_blocks(aq_l), to_blocks(bq_l)))) @ w_out
    if not with_ctx:
        return out_l, None
    out_c = jnp.concatenate([diff_attn_core(aq_c, ak_c, av_c, lam, lam_init, sub_g),
                             gqa_core(bq_c, bk_c, bv_c)], axis=-1) @ w_out
    return out_l, out_c


def odd_mixer(h_lat, h_ctx, w_in, w_out, rpb, nbr_idx, bias_idx, with_ctx):
    b = h_lat.shape[0]

    def project(h):
        s = h.shape[1]
        q, k, v = jnp.split(h @ w_in, 3, axis=-1)
        return (q.reshape(b, s, C_HEADS, HEAD_DIM), k.reshape(b, s, C_HEADS, HEAD_DIM),
                v.reshape(b, s, C_HEADS, HEAD_DIM))

    q_l, k_l, v_l = project(h_lat)
    q_c, k_c, v_c = project(h_ctx)
    rpb_flat = rpb.reshape(C_HEADS, RPB_H * RPB_W)
    scale = HEAD_DIM ** -0.5
    n_nbr = nbr_idx.shape[1]

    def block(args):
        q_b, idx_b, bidx_b = args
        k_nb = k_l[:, idx_b]
        v_nb = v_l[:, idx_b]
        s_nb = jnp.einsum('bqhd,bqnhd->bhqn', q_b, k_nb) * scale + rpb_flat[:, bidx_b][None]
        s_cx = jnp.einsum('bqhd,bkhd->bhqk', q_b, k_c) * scale
        p = softmax_f32(jnp.concatenate([s_nb, s_cx], axis=-1)).astype(v_l.dtype)
        o = (jnp.einsum('bhqn,bqnhd->bqhd', p[..., :n_nbr], v_nb)
             + jnp.einsum('bhqk,bkhd->bqhd', p[..., n_nbr:], v_c))
        return o.reshape(b, q_b.shape[1], C_WIDTH)

    nb = nbr_idx.shape[0] // Q_BLOCK
    out_l = from_blocks(lax.map(block, (to_blocks(q_l),
                                        nbr_idx.reshape(nb, Q_BLOCK, n_nbr),
                                        bias_idx.reshape(nb, Q_BLOCK, n_nbr)))) @ w_out
    if not with_ctx:
        return out_l, None
    s = jnp.einsum('bqhd,bkhd->bhqk', q_c, k_c) * scale
    p = softmax_f32(s).astype(v_c.dtype)
    out_c = jnp.einsum('bhqk,bkhd->bqhd', p, v_c).reshape(b, q_c.shape[1], C_WIDTH) @ w_out
    return out_l, out_c


def hier_moe(h, router_g, router_e, w1, w3, w2):
    n_tok = h.shape[0]
    lg = (h @ router_g).astype(jnp.float32)
    pg = jax.nn.softmax(lg, axis=-1)
    g_star = jnp.argmax(lg, axis=-1).astype(jnp.int32)
    p_top = jnp.take_along_axis(pg, g_star[:, None], axis=1)
    le = (h @ router_e).astype(jnp.float32).reshape(n_tok, N_GROUPS, EXPERTS_PER_GROUP)
    le_g = jnp.take_along_axis(le, g_star[:, None, None], axis=1)[:, 0]
    top_v, top_i = lax.top_k(le_g, TOP_K_INNER)
    gate = jax.nn.softmax(top_v, axis=-1) * p_top
    expert = g_star[:, None] * EXPERTS_PER_GROUP + top_i.astype(jnp.int32)
    n_assign = n_tok * TOP_K_INNER
    e_flat = expert.reshape(-1)
    tok_flat = jnp.repeat(jnp.arange(n_tok, dtype=jnp.int32), TOP_K_INNER)
    w_flat = gate.reshape(-1)
    order = jnp.argsort(e_flat)
    e_s, tok_s, w_s = e_flat[order], tok_flat[order], w_flat[order]
    counts = jnp.zeros((N_EXPERTS,), jnp.int32).at[e_flat].add(1)
    starts = jnp.cumsum(counts) - counts
    padded = (counts + MOE_BLOCK - 1) // MOE_BLOCK * MOE_BLOCK
    pad_ends = jnp.cumsum(padded)
    pad_starts = pad_ends - padded
    dest = pad_starts[e_s] + jnp.arange(n_assign, dtype=jnp.int32) - starts[e_s]
    n_blocks = (n_assign + MOE_BLOCK - 1) // MOE_BLOCK + N_EXPERTS
    n_rows = n_blocks * MOE_BLOCK
    row_tok = jnp.zeros((n_rows,), jnp.int32).at[dest].set(tok_s)
    row_w = jnp.zeros((n_rows,), jnp.float32).at[dest].set(w_s)
    block_exp = jnp.minimum(jnp.searchsorted(pad_ends, jnp.arange(n_blocks, dtype=jnp.int32) * MOE_BLOCK,
                                             side='right'), N_EXPERTS - 1).astype(jnp.int32)

    def expert_block(args):
        tok_b, e = args
        xb = h[tok_b]
        return (jax.nn.silu(xb @ w1[e]) * (xb @ w3[e])) @ w2[e]

    y = lax.map(expert_block, (row_tok.reshape(n_blocks, MOE_BLOCK), block_exp))
    y = y.reshape(n_rows, h.shape[1]) * row_w[:, None].astype(h.dtype)
    return jnp.zeros_like(h).at[row_tok].add(y)


def setup_inputs(seed: int = 0) -> dict:
    key = jax.random.key(seed)
    ks = jax.random.split(key, 21)
    D = D_MODEL

    def nrm(k, shape, scale):
        return jax.random.normal(k, shape, jnp.float32) * scale

    return {
        "x": nrm(ks[0], (BATCH, SEQ, D), 1.0),
        "c": nrm(ks[1], (BATCH, D), 1.0),
        "ctx": nrm(ks[2], (BATCH, CTX_LEN, D), 1.0),
        "c_ctx": nrm(ks[3], (D,), 1.0),
        "w_mod": nrm(ks[4], (DEPTH, D, 6 * D), 0.5 * D ** -0.5),
        "b_mod": nrm(ks[5], (DEPTH, 6 * D), 0.01),
        "ln_g": 1.0 + nrm(ks[6], (DEPTH, 2, D), 0.02),
        "ln_b": nrm(ks[7], (DEPTH, 2, D), 0.01),
        "w_in_even": nrm(ks[8], (N_EVEN, D, EVEN_IN), D ** -0.5),
        "w_out_even": nrm(ks[9], (N_EVEN, EVEN_OUT, D), DN_BETA * EVEN_OUT ** -0.5),
        "diff_lam": nrm(ks[10], (N_EVEN, 4, HEAD_DIM), 0.1),
        "diff_subln_g": 1.0 + nrm(ks[11], (N_EVEN, A_VDIM), 0.02),
        "gqa_qk_g": 1.0 + nrm(ks[12], (N_EVEN, 2, HEAD_DIM), 0.02),
        "w_in_odd": nrm(ks[13], (N_ODD, D, ODD_IN), D ** -0.5),
        "w_out_odd": nrm(ks[14], (N_ODD, C_WIDTH, D), DN_BETA * C_WIDTH ** -0.5),
        "na_rpb": nrm(ks[15], (N_ODD, C_HEADS, RPB_H, RPB_W), 0.02),
        "router_g": nrm(ks[16], (DEPTH, D, N_GROUPS), D ** -0.5),
        "router_e": nrm(ks[17], (DEPTH, D, N_EXPERTS), D ** -0.5),
        "w1": nrm(ks[18], (DEPTH, N_EXPERTS, D, D_EXPERT), D ** -0.5),
        "w3": nrm(ks[19], (DEPTH, N_EXPERTS, D, D_EXPERT), D ** -0.5),
        "w2": nrm(ks[20], (DEPTH, N_EXPERTS, D_EXPERT, D), DN_BETA * D_EXPERT ** -0.5),
    }


def reference(x, c, ctx, c_ctx, w_mod, b_mod, ln_g, ln_b, w_in_even, w_out_even, diff_lam, diff_subln_g,
              gqa_qk_g, w_in_odd, w_out_odd, na_rpb, router_g, router_e, w1, w3, w2):
    b, n, d = x.shape
    n_ctx = ctx.shape[1]
    cos, sin = axial_rope_tables(n)
    nbr_idx, bias_idx = neighbourhood_tables(n)
    s_c = jax.nn.silu(c)
    s_cc = jax.nn.silu(c_ctx)
    for l in range(DEPTH):
        last = l == DEPTH - 1
        i = l // 2
        mod = (s_c @ w_mod[l] + b_mod[l])[:, None, :]
        mod_c = s_cc @ w_mod[l] + b_mod[l]
        sh1, sc1, g1, sh2, sc2, g2 = jnp.split(mod, 6, axis=-1)
        csh1, csc1, cg1, csh2, csc2, cg2 = jnp.split(mod_c, 6, axis=-1)
        h_l = x * (1.0 + sc1) + sh1
        h_c = ctx * (1.0 + csc1) + csh1
        if l % 2 == 0:
            lam_init = 0.8 - 0.6 * math.exp(-0.3 * l)
            mix_l, mix_c = even_mixer(h_l, h_c, w_in_even[i], w_out_even[i], diff_lam[i], diff_subln_g[i],
                                      gqa_qk_g[i], lam_init, cos, sin, not last)
        else:
            mix_l, mix_c = odd_mixer(h_l, h_c, w_in_odd[i], w_out_odd[i], na_rpb[i], nbr_idx, bias_idx, not last)
        x = layer_norm(DN_ALPHA * x + g1 * mix_l, ln_g[l, 0], ln_b[l, 0])
        if not last:
            ctx = layer_norm(DN_ALPHA * ctx + cg1 * mix_c, ln_g[l, 0], ln_b[l, 0])
        h2_l = (x * (1.0 + sc2) + sh2).reshape(b * n, d)
        if last:
            y_l = hier_moe(h2_l, router_g[l], router_e[l], w1[l], w3[l], w2[l]).reshape(b, n, d)
        else:
            h2_c = (ctx * (1.0 + csc2) + csh2).reshape(b * n_ctx, d)
            y = hier_moe(jnp.concatenate([h2_c, h2_l], axis=0), router_g[l], router_e[l], w1[l], w3[l], w2[l])
            y_l = y[b * n_ctx:].reshape(b, n, d)
            ctx = layer_norm(DN_ALPHA * ctx + cg2 * y[:b * n_ctx].reshape(b, n_ctx, d), ln_g[l, 1], ln_b[l, 1])
        x = layer_norm(DN_ALPHA * x + g2 * y_l, ln_g[l, 1], ln_b[l, 1])
    return x
```

```python
import functools
import math

import numpy as np
import jax
import jax.numpy as jnp
from jax import lax
from jax.experimental import pallas as pl
from jax.experimental.pallas import tpu as pltpu

D_MODEL = 1024
DEPTH = 4
GRID_W = 64
CTX_LEN = 256
HEAD_DIM = 64
A_HEADS = 4
B_HEADS = 8
B_KV_HEADS = 2
C_HEADS = D_MODEL // HEAD_DIM
NA_WIN_H = 8
NA_WIN_W = 16
RPB_H = 2 * NA_WIN_H - 1
RPB_W = 2 * NA_WIN_W - 1
N_GROUPS = 4
EXPERTS_PER_GROUP = 8
N_EXPERTS = N_GROUPS * EXPERTS_PER_GROUP
D_EXPERT = 512
ROPE_THETA = 10000.0
DN_ALPHA = (2.0 * DEPTH) ** 0.25
EVEN_IN = 2304
LN_EPS = 1e-6

LANES = 128
ROW_BLK = 256
MOE_BLK = 256
N_MAPS = 16
N_KV_CHUNKS = 5
BAND_ROWS = 10
NA_QBLK = 2 * GRID_W
NEG = -1e30
VMEM_LIMIT = 48 * 1024 * 1024

F32 = jnp.float32
BF16 = jnp.bfloat16
HIGHEST = lax.Precision.HIGHEST


def _cparams(n_axes):
    return pltpu.CompilerParams(dimension_semantics=("arbitrary",) * n_axes,
                                vmem_limit_bytes=VMEM_LIMIT)


def _dot_nt(a, b):
    return lax.dot_general(a, b, (((1,), (1,)), ((), ())), preferred_element_type=F32)


def _layer_norm(v, g, b):
    mu = jnp.mean(v, axis=-1, keepdims=True)
    d = v - mu
    var = jnp.mean(d * d, axis=-1, keepdims=True)
    return d * lax.rsqrt(var + LN_EPS) * g + b


def _silu(v):
    return v / (1.0 + jnp.exp(-v))


def _mod_kernel(c_ref, w_ref, b_ref, o_ref):
    s = _silu(c_ref[...])
    o_ref[...] = jnp.dot(s, w_ref[...], preferred_element_type=F32, precision=HIGHEST) + b_ref[...]


def _modulation(cvec, w_mod, b_mod):
    d = D_MODEL
    return pl.pallas_call(
        _mod_kernel,
        grid=(DEPTH, 6),
        in_specs=[pl.BlockSpec((8, d), lambda l, j: (0, 0)),
                  pl.BlockSpec((None, d, d), lambda l, j: (l, 0, j)),
                  pl.BlockSpec((None, 1, d), lambda l, j: (l, 0, j))],
        out_specs=pl.BlockSpec((None, 8, d), lambda l, j: (l, 0, j)),
        out_shape=jax.ShapeDtypeStruct((DEPTH, 8, 6 * d), F32),
        compiler_params=_cparams(2),
        name="modulation",
    )(cvec, w_mod, b_mod.reshape(DEPTH, 1, 6 * d))


def _mod_spec(blk0):
    return pl.BlockSpec((None, 6, D_MODEL), lambda i: (jnp.where(i + blk0 == 0, 1, 0), 0, 0))


def _rope(z, cos, sin, first16):
    rot = jnp.where(first16, -pltpu.roll(z, LANES - 16, 1), pltpu.roll(z, 16, 1))
    return z * cos + rot * sin


def _proj_even_kernel(x_ref, mod_ref, w_ref, cos_ref, sin_ref, g_ref, hm_ref, q_ref, k_ref, v_ref):
    x = x_ref[...]
    h = (x * (1.0 + mod_ref[1:2, :]) + mod_ref[0:1, :]).astype(BF16)
    y = jnp.dot(h, w_ref[...], preferred_element_type=F32)
    cos = cos_ref[...]
    sin = sin_ref[...]
    lane = lax.broadcasted_iota(jnp.int32, cos.shape, 1)
    first16 = (lane % 32) < 16
    lo = lane < HEAD_DIM
    hm = hm_ref[...]
    scale = HEAD_DIM ** -0.5

    def chunk(c):
        return y[:, c * LANES:(c + 1) * LANES]

    def qk_norm(z, g):
        ms = jnp.dot(z * z, hm, preferred_element_type=F32, precision=HIGHEST)
        return z * lax.rsqrt(ms + LN_EPS) * g

    for hd in range(A_HEADS):
        q = _rope(chunk(hd), cos, sin, first16) * scale
        q_ref[hd] = jnp.where(lo, q, 0.0).astype(BF16)
        q_ref[A_HEADS + hd] = jnp.where(lo, 0.0, q).astype(BF16)
        k_ref[hd] = _rope(chunk(4 + hd), cos, sin, first16).astype(BF16)
        v_ref[hd] = chunk(8 + hd).astype(BF16)
    gq = g_ref[0:1, :]
    gk = g_ref[1:2, :]
    for pr in range(B_HEADS // 2):
        q = _rope(qk_norm(chunk(12 + pr), gq), cos, sin, first16) * scale
        qs = pltpu.roll(q, HEAD_DIM, 1)
        if pr < 2:
            q_ref[8 + 2 * pr] = jnp.where(lo, q, 0.0).astype(BF16)
            q_ref[8 + 2 * pr + 1] = jnp.where(lo, qs, 0.0).astype(BF16)
        else:
            q_ref[8 + 2 * pr] = jnp.where(lo, 0.0, qs).astype(BF16)
            q_ref[8 + 2 * pr + 1] = jnp.where(lo, 0.0, q).astype(BF16)
    k_ref[4] = _rope(qk_norm(chunk(16), gk), cos, sin, first16).astype(BF16)
    v_ref[4] = chunk(17).astype(BF16)


def _proj_even(xt, modl, w_in, cos, sin, g2, hm):
    t = xt.shape[0]
    nb = t // ROW_BLK
    return pl.pallas_call(
        _proj_even_kernel,
        grid=(nb,),
        in_specs=[pl.BlockSpec((ROW_BLK, D_MODEL), lambda i: (i, 0)),
                  _mod_spec(0),
                  pl.BlockSpec((D_MODEL, EVEN_IN), lambda i: (0, 0)),
                  pl.BlockSpec((ROW_BLK, LANES), lambda i: (i, 0)),
                  pl.BlockSpec((ROW_BLK, LANES), lambda i: (i, 0)),
                  pl.BlockSpec((2, LANES), lambda i: (0, 0)),
                  pl.BlockSpec((LANES, LANES), lambda i: (0, 0))],
        out_specs=[pl.BlockSpec((N_MAPS, ROW_BLK, LANES), lambda i: (0, i, 0)),
                   pl.BlockSpec((N_KV_CHUNKS, ROW_BLK, LANES), lambda i: (0, i, 0)),
                   pl.BlockSpec((N_KV_CHUNKS, ROW_BLK, LANES), lambda i: (0, i, 0))],
        out_shape=[jax.ShapeDtypeStruct((N_MAPS, t, LANES), BF16),
                   jax.ShapeDtypeStruct((N_KV_CHUNKS, t, LANES), BF16),
                   jax.ShapeDtypeStruct((N_KV_CHUNKS, t, LANES), BF16)],
        compiler_params=_cparams(1),
        name="proj_even",
    )(xt, modl, w_in, cos, sin, g2, hm)


def _attn_even_kernel(lam_ref, subg_ref, q_ref, k_ref, v_ref, o_ref, m_sc, l_sc, acc_sc, *, nkv, bk, lam_init):
    j = pl.program_id(1)

    @pl.when(j == 0)
    def _():
        m_sc[...] = jnp.full(m_sc.shape, NEG, F32)
        l_sc[...] = jnp.zeros(l_sc.shape, F32)
        acc_sc[...] = jnp.zeros(acc_sc.shape, F32)

    nchunk = bk // LANES

    def one_map(mm, carry):
        kidx = jnp.where(mm < 2 * A_HEADS, lax.rem(mm, A_HEADS), A_HEADS)
        s = _dot_nt(q_ref[mm], k_ref[kidx])
        m_prev = m_sc[mm]
        cm = s[:, 0:LANES]
        for c in range(1, nchunk):
            cm = jnp.maximum(cm, s[:, c * LANES:(c + 1) * LANES])
        m_new = jnp.maximum(m_prev, jnp.max(cm, axis=1, keepdims=True))
        alpha = jnp.exp(m_prev - m_new)
        ps = []
        lsum = None
        for c in range(nchunk):
            p = jnp.exp(s[:, c * LANES:(c + 1) * LANES] - m_new)
            lsum = p if lsum is None else lsum + p
            ps.append(p.astype(BF16))
        pm = jnp.concatenate(ps, axis=1) if nchunk > 1 else ps[0]
        l_sc[mm] = alpha * l_sc[mm] + jnp.sum(lsum, axis=1, keepdims=True)
        acc_sc[mm] = alpha * acc_sc[mm] + jnp.dot(pm, v_ref[kidx], preferred_element_type=F32)
        m_sc[mm] = m_new
        return carry

    lax.fori_loop(0, N_MAPS, one_map, 0)

    @pl.when(j == nkv - 1)
    def _():
        lamv = lam_ref[...]
        e1 = jnp.exp(jnp.sum(lamv[0:1, :] * lamv[1:2, :], axis=1, keepdims=True))
        e2 = jnp.exp(jnp.sum(lamv[2:3, :] * lamv[3:4, :], axis=1, keepdims=True))
        lam = e1 - e2 + lam_init
        subg = subg_ref[...]
        for hd in range(A_HEADS):
            o = acc_sc[hd] / l_sc[hd] - lam * (acc_sc[A_HEADS + hd] / l_sc[A_HEADS + hd])
            ms = jnp.mean(o * o, axis=-1, keepdims=True)
            o = o * lax.rsqrt(ms + LN_EPS) * subg * (1.0 - lam_init)
            o_ref[:, hd * LANES:(hd + 1) * LANES] = o.astype(BF16)
        lane = lax.broadcasted_iota(jnp.int32, (o_ref.shape[0], LANES), 1)
        lo = lane < HEAD_DIM
        for pr in range(B_HEADS // 2):
            oa = acc_sc[8 + 2 * pr] / l_sc[8 + 2 * pr]
            ob = acc_sc[8 + 2 * pr + 1] / l_sc[8 + 2 * pr + 1]
            if pr < 2:
                o = jnp.where(lo, oa, pltpu.roll(ob, HEAD_DIM, 1))
            else:
                o = jnp.where(lo, pltpu.roll(oa, HEAD_DIM, 1), ob)
            o_ref[:, (A_HEADS + pr) * LANES:(A_HEADS + pr + 1) * LANES] = o.astype(BF16)


def _attn_even(q, k, v, lamf, subg, *, q_row0, n_q, n_k, bq, bk, lam_init):
    nq = n_q // bq
    nkv = n_k // bk
    qoff = q_row0 // bq
    kern = functools.partial(_attn_even_kernel, nkv=nkv, bk=bk, lam_init=lam_init)
    return pl.pallas_call(
        kern,
        grid=(nq, nkv),
        in_specs=[pl.BlockSpec((4, HEAD_DIM), lambda i, j: (0, 0)),
                  pl.BlockSpec((1, LANES), lambda i, j: (0, 0)),
                  pl.BlockSpec((N_MAPS, bq, LANES), lambda i, j: (0, i + qoff, 0)),
                  pl.BlockSpec((N_KV_CHUNKS, bk, LANES), lambda i, j: (0, j, 0)),
                  pl.BlockSpec((N_KV_CHUNKS, bk, LANES), lambda i, j: (0, j, 0))],
        out_specs=pl.BlockSpec((bq, D_MODEL), lambda i, j: (i, 0)),
        out_shape=jax.ShapeDtypeStruct((n_q, D_MODEL), BF16),
        scratch_shapes=[pltpu.VMEM((N_MAPS, bq, LANES), F32)] * 3,
        compiler_params=_cparams(2),
        name="attn_even",
    )(lamf, subg, q, k, v)


def _proj_odd_kernel(x_ref, mod_ref, w_ref, q_ref, k_ref, v_ref):
    x = x_ref[...]
    h = (x * (1.0 + mod_ref[1:2, :]) + mod_ref[0:1, :]).astype(BF16)
    d = D_MODEL
    q_ref[...] = (jnp.dot(h, w_ref[:, 0:d], preferred_element_type=F32) * (HEAD_DIM ** -0.5)).astype(BF16)
    k_ref[...] = jnp.dot(h, w_ref[:, d:2 * d], preferred_element_type=F32).astype(BF16)
    v_ref[...] = jnp.dot(h, w_ref[:, 2 * d:3 * d], preferred_element_type=F32).astype(BF16)


def _proj_odd(xt, modl, w_in):
    t = xt.shape[0]
    d = D_MODEL
    spec = pl.BlockSpec((ROW_BLK, d), lambda i: (i, 0))
    return pl.pallas_call(
        _proj_odd_kernel,
        grid=(t // ROW_BLK,),
        in_specs=[spec, _mod_spec(0), pl.BlockSpec((d, 3 * d), lambda i: (0, 0))],
        out_specs=[spec, spec, spec],
        out_shape=[jax.ShapeDtypeStruct((t, d), BF16)] * 3,
        compiler_params=_cparams(1),
        name="proj_odd",
    )(xt, modl, w_in)


def _softmax_av(parts):
    m = None
    for s, _ in parts:
        sm = jnp.max(s, axis=1, keepdims=True)
        m = sm if m is None else jnp.maximum(m, sm)
    l = None
    o = None
    for s, v in parts:
        p = jnp.exp(s - m)
        ps = jnp.sum(p, axis=1, keepdims=True)
        pv = jnp.dot(p.astype(BF16), v, preferred_element_type=F32)
        l = ps if l is None else l + ps
        o = pv if o is None else o + pv
    return o / l


def _natten_kernel(q_ref, k_ref, v_ref, bm_ref, o_ref, *, rows):
    b = pl.program_id(1)
    nb = pl.num_programs(1)
    band = jnp.clip(2 * b - NA_WIN_H // 2, 0, rows - BAND_ROWS)
    typ = jnp.where(b < 2, b, jnp.where(b >= nb - 2, b - (nb - 5), 2))
    start = pl.multiple_of(CTX_LEN + band * GRID_W, GRID_W)
    nband = BAND_ROWS * GRID_W
    kb = k_ref[pl.ds(start, nband), :]
    vb = v_ref[pl.ds(start, nband), :]
    kc = k_ref[0:CTX_LEN, :]
    vc = v_ref[0:CTX_LEN, :]
    q = q_ref[...]
    lane = lax.broadcasted_iota(jnp.int32, q.shape, 1)
    lo = lane < HEAD_DIM
    zero = jnp.zeros_like(q)
    outs = []
    for half in range(2):
        qh = jnp.where(lo, q, zero) if half == 0 else jnp.where(lo, zero, q)
        s_nb = _dot_nt(qh, kb) + bm_ref[typ, half]
        s_cx = _dot_nt(qh, kc)
        outs.append(_softmax_av([(s_nb, vb), (s_cx, vc)]))
    o_ref[...] = jnp.where(lo, outs[0], outs[1]).astype(BF16)


def _natten(q, k, v, bm, n):
    t = q.shape[0]
    rows = n // GRID_W
    nqb = n // NA_QBLK
    qoff = CTX_LEN // NA_QBLK
    npair = C_HEADS // 2
    kern = functools.partial(_natten_kernel, rows=rows)
    return pl.pallas_call(
        kern,
        grid=(npair, nqb),
        in_specs=[pl.BlockSpec((NA_QBLK, LANES), lambda j, b: (b + qoff, j)),
                  pl.BlockSpec((t, LANES), lambda j, b: (0, j)),
                  pl.BlockSpec((t, LANES), lambda j, b: (0, j)),
                  pl.BlockSpec((5, 2, NA_QBLK, BAND_ROWS * GRID_W), lambda j, b: (0, j, 0, 0))],
        out_specs=pl.BlockSpec((NA_QBLK, LANES), lambda j, b: (b, j)),
        out_shape=jax.ShapeDtypeStruct((n, D_MODEL), BF16),
        compiler_params=_cparams(2),
        name="natten",
    )(q, k, v, bm)


def _ctx_attn_kernel(q_ref, k_ref, v_ref, o_ref):
    q = q_ref[...]
    k = k_ref[...]
    v = v_ref[...]
    lane = lax.broadcasted_iota(jnp.int32, q.shape, 1)
    lo = lane < HEAD_DIM
    zero = jnp.zeros_like(q)
    outs = []
    for half in range(2):
        qh = jnp.where(lo, q, zero) if half == 0 else jnp.where(lo, zero, q)
        outs.append(_softmax_av([(_dot_nt(qh, k), v)]))
    o_ref[...] = jnp.where(lo, outs[0], outs[1]).astype(BF16)


def _ctx_attn(q, k, v):
    spec = pl.BlockSpec((CTX_LEN, LANES), lambda j: (0, j))
    return pl.pallas_call(
        _ctx_attn_kernel,
        grid=(C_HEADS // 2,),
        in_specs=[spec, spec, spec],
        out_specs=spec,
        out_shape=jax.ShapeDtypeStruct((CTX_LEN, D_MODEL), BF16),
        compiler_params=_cparams(1),
        name="ctx_attn",
    )(q, k, v)


def _natten_bias_tables(rpb, n):
    rows = n // GRID_W
    nb = rows // 2
    r128 = jnp.zeros((C_HEADS, RPB_H, LANES), F32)
    r128 = r128.at[:, :, 0:NA_WIN_W].set(rpb[:, :, NA_WIN_W - 1:])
    r128 = r128.at[:, :, LANES - (NA_WIN_W - 1):].set(rpb[:, :, 0:NA_WIN_W - 1])
    toep = jnp.tile(r128, (1, 1, GRID_W))[:, :, :GRID_W * (LANES - 1)]
    toep = toep.reshape(C_HEADS, RPB_H, GRID_W, LANES - 1)[:, :, :, :GRID_W]
    c = np.arange(GRID_W)
    cs = np.clip(c - NA_WIN_W // 2, 0, GRID_W - NA_WIN_W)
    kc = np.arange(GRID_W)
    col_ok = (kc[None, :] >= cs[:, None]) & (kc[None, :] < cs[:, None] + NA_WIN_W)
    colmask = jnp.asarray(np.where(col_ok, 0.0, NEG), F32)
    negblk = jnp.full((C_HEADS, GRID_W, GRID_W), NEG, F32)
    tables = []
    for b in (0, 1, 2, nb - 2, nb - 1):
        band = int(np.clip(2 * b - NA_WIN_H // 2, 0, rows - BAND_ROWS))
        qrows = []
        for qr in range(2):
            r = 2 * b + qr
            rs = int(np.clip(r - NA_WIN_H // 2, 0, rows - NA_WIN_H))
            blks = []
            for i in range(BAND_ROWS):
                kr = band + i
                if rs <= kr < rs + NA_WIN_H:
                    blks.append(toep[:, kr - r + NA_WIN_H - 1] + colmask[None])
                else:
                    blks.append(negblk)
            qrows.append(jnp.concatenate(blks, axis=2))
        tables.append(jnp.concatenate(qrows, axis=1))
    return jnp.stack(tables, axis=0)


def _post_attn_kernel(*refs, with_ctx, blk0):
    if with_ctx:
        octx_ref, refs = refs[0], refs[1:]
    olat_ref, x_ref, mod_ref, wout_ref, lng_ref, lnb_ref, rw_ref, x1_ref, h2_ref, r_ref = refs
    o = olat_ref[...]
    if with_ctx:
        o = jnp.where(pl.program_id(0) + blk0 == 0, octx_ref[...], o)
    mix = jnp.dot(o, wout_ref[...], preferred_element_type=F32)
    x1 = _layer_norm(DN_ALPHA * x_ref[...] + mod_ref[2:3, :] * mix, lng_ref[0:1, :], lnb_ref[0:1, :])
    x1_ref[...] = x1
    h2 = x1 * (1.0 + mod_ref[4:5, :]) + mod_ref[3:4, :]
    h2_ref[...] = h2
    logits = jnp.dot(h2, rw_ref[...], preferred_element_type=F32, precision=HIGHEST)
    lane = lax.broadcasted_iota(jnp.int32, logits.shape, 1).astype(F32)
    big = float(LANES)
    lg = jnp.where(lane < N_GROUPS, logits, NEG)
    mg = jnp.max(lg, axis=1, keepdims=True)
    gstar = jnp.min(jnp.where(lg == mg, lane, big), axis=1, keepdims=True)
    p_top = 1.0 / jnp.sum(jnp.exp(lg - mg), axis=1, keepdims=True)
    e_lo = N_GROUPS + EXPERTS_PER_GROUP * gstar
    le = jnp.where((lane >= e_lo) & (lane < e_lo + EXPERTS_PER_GROUP), logits, NEG)
    v1 = jnp.max(le, axis=1, keepdims=True)
    i1 = jnp.min(jnp.where(le == v1, lane, big), axis=1, keepdims=True)
    le2 = jnp.where(lane == i1, NEG, le)
    v2 = jnp.max(le2, axis=1, keepdims=True)
    i2 = jnp.min(jnp.where(le2 == v2, lane, big), axis=1, keepdims=True)
    e2 = jnp.exp(v2 - v1)
    w1 = p_top / (1.0 + e2)
    w2 = p_top * e2 / (1.0 + e2)
    r = jnp.where(lane == 0.0, i1 - N_GROUPS, 0.0)
    r = jnp.where(lane == 1.0, i2 - N_GROUPS, r)
    r = jnp.where(lane == 2.0, w1, r)
    r = jnp.where(lane == 3.0, w2, r)
    r_ref[...] = r


def _post_attn(o_ctx, o_lat, xt, modl, w_out, lng, lnb, rw, *, with_ctx):
    t = xt.shape[0]
    d = D_MODEL
    blk0 = 0 if with_ctx else CTX_LEN // ROW_BLK
    nblk = t // ROW_BLK - blk0
    lat0 = CTX_LEN // ROW_BLK
    row = pl.BlockSpec((ROW_BLK, d), lambda i: (i + blk0, 0))
    orow = pl.BlockSpec((ROW_BLK, d), lambda i: (i, 0))
    in_specs = [pl.BlockSpec((ROW_BLK, d), lambda i: (jnp.maximum(i + blk0 - lat0, 0), 0)),
                row, _mod_spec(blk0),
                pl.BlockSpec((d, d), lambda i: (0, 0)),
                pl.BlockSpec((2, d), lambda i: (0, 0)),
                pl.BlockSpec((2, d), lambda i: (0, 0)),
                pl.BlockSpec((d, LANES), lambda i: (0, 0))]
    args = [o_lat, xt, modl, w_out, lng, lnb, rw]
    if with_ctx:
        in_specs = [pl.BlockSpec((ROW_BLK, d), lambda i: (0, 0))] + in_specs
        args = [o_ctx] + args
    kern = functools.partial(_post_attn_kernel, with_ctx=with_ctx, blk0=blk0)
    return pl.pallas_call(
        kern,
        grid=(nblk,),
        in_specs=in_specs,
        out_specs=[orow, orow, pl.BlockSpec((ROW_BLK, LANES), lambda i: (i, 0))],
        out_shape=[jax.ShapeDtypeStruct((nblk * ROW_BLK, d), F32),
                   jax.ShapeDtypeStruct((nblk * ROW_BLK, d), F32),
                   jax.ShapeDtypeStruct((nblk * ROW_BLK, LANES), F32)],
        compiler_params=_cparams(1),
        name="post_attn",
    )(*args)


def _row_copies(n_rows, make):
    def start(r, c):
        for kk in range(2):
            make(r, kk).start()
        return c

    def wait(r, c):
        for kk in range(2):
            make(r, kk).wait()
        return c

    lax.fori_loop(0, n_rows, start, 0)
    lax.fori_loop(0, n_rows, wait, 0)


def _dispatch_kernel(dest_ref, h_ref, xs_in_ref, xs_ref, sem):
    del xs_in_ref

    def make(r, kk):
        dst = dest_ref[0, 0, 2 * r + kk]
        return pltpu.make_async_copy(h_ref.at[pl.ds(r, 1)], xs_ref.at[pl.ds(dst, 1)], sem)

    _row_copies(h_ref.shape[0], make)


def _dispatch(h2, dest3, xs0):
    nblk = dest3.shape[0]
    return pl.pallas_call(
        _dispatch_kernel,
        grid=(nblk,),
        in_specs=[pl.BlockSpec((1, 1, 2 * ROW_BLK), lambda i: (i, 0, 0), memory_space=pltpu.SMEM),
                  pl.BlockSpec((ROW_BLK, D_MODEL), lambda i: (i, 0)),
                  pl.BlockSpec(memory_space=pl.ANY)],
        out_specs=pl.BlockSpec(memory_space=pl.ANY),
        out_shape=jax.ShapeDtypeStruct(xs0.shape, xs0.dtype),
        scratch_shapes=[pltpu.SemaphoreType.DMA(())],
        input_output_aliases={2: 0},
        compiler_params=pltpu.CompilerParams(dimension_semantics=("arbitrary",),
                                             vmem_limit_bytes=VMEM_LIMIT, has_side_effects=True),
        name="moe_dispatch",
    )(dest3, h2, xs0)


def _experts_kernel(bexp_ref, nused_ref, xs_ref, w1_ref, w3_ref, w2_ref, y_ref):
    del bexp_ref
    b = pl.program_id(0)

    @pl.when(b < nused_ref[0])
    def _():
        x = xs_ref[...].astype(BF16)
        h1 = jnp.dot(x, w1_ref[...].astype(BF16), preferred_element_type=F32)
        h3 = jnp.dot(x, w3_ref[...].astype(BF16), preferred_element_type=F32)
        a = (_silu(h1) * h3).astype(BF16)
        y_ref[...] = jnp.dot(a, w2_ref[...].astype(BF16), preferred_element_type=F32)

    @pl.when(b >= nused_ref[0])
    def _():
        y_ref[...] = jnp.zeros(y_ref.shape, F32)


def _experts(block_exp, nused, xs, w1, w3, w2):
    n_rows = xs.shape[0]
    d = D_MODEL
    grid_spec = pltpu.PrefetchScalarGridSpec(
        num_scalar_prefetch=2,
        grid=(n_rows // MOE_BLK,),
        in_specs=[pl.BlockSpec((MOE_BLK, d), lambda b, be, nu: (jnp.minimum(b, nu[0] - 1), 0)),
                  pl.BlockSpec((None, d, D_EXPERT), lambda b, be, nu: (be[b], 0, 0)),
                  pl.BlockSpec((None, d, D_EXPERT), lambda b, be, nu: (be[b], 0, 0)),
                  pl.BlockSpec((None, D_EXPERT, d), lambda b, be, nu: (be[b], 0, 0))],
        out_specs=pl.BlockSpec((MOE_BLK, d), lambda b, be, nu: (b, 0)),
    )
    return pl.pallas_call(
        _experts_kernel,
        grid_spec=grid_spec,
        out_shape=jax.ShapeDtypeStruct((n_rows, d), F32),
        compiler_params=_cparams(1),
        name="moe_experts",
    )(block_exp, nused, xs, w1, w3, w2)


def _combine_kernel(dest_ref, x1_ref, r_ref, mod_ref, lng_ref, lnb_ref, y_ref, o_ref, ybuf, sem):
    def make(r, kk):
        src = dest_ref[0, 0, 2 * r + kk]
        return pltpu.make_async_copy(y_ref.at[pl.ds(src, 1)], ybuf.at[kk, pl.ds(r, 1)], sem)

    _row_copies(x1_ref.shape[0], make)
    ymix = ybuf[0] * r_ref[:, 2:3] + ybuf[1] * r_ref[:, 3:4]
    o_ref[...] = _layer_norm(DN_ALPHA * x1_ref[...] + mod_ref[5:6, :] * ymix, lng_ref[1:2, :], lnb_ref[1:2, :])


def _combine(dest3, x1, r, modl, lng, lnb, y, *, blk0):
    nblk = dest3.shape[0]
    d = D_MODEL
    return pl.pallas_call(
        _combine_kernel,
        grid=(nblk,),
        in_specs=[pl.BlockSpec((1, 1, 2 * ROW_BLK), lambda i: (i, 0, 0), memory_space=pltpu.SMEM),
                  pl.BlockSpec((ROW_BLK, d), lambda i: (i, 0)),
                  pl.BlockSpec((ROW_BLK, LANES), lambda i: (i, 0)),
                  _mod_spec(blk0),
                  pl.BlockSpec((2, d), lambda i: (0, 0)),
                  pl.BlockSpec((2, d), lambda i: (0, 0)),
                  pl.BlockSpec(memory_space=pl.ANY)],
        out_specs=pl.BlockSpec((ROW_BLK, d), lambda i: (i, 0)),
        out_shape=jax.ShapeDtypeStruct((nblk * ROW_BLK, d), F32),
        scratch_shapes=[pltpu.VMEM((2, ROW_BLK, d), F32), pltpu.SemaphoreType.DMA(())],
        compiler_params=_cparams(1),
        name="moe_combine",
    )(dest3, x1, r, modl, lng, lnb, y)


def _dispatch_plan(r):
    ntok = r.shape[0]
    ef = r[:, 0:2].astype(jnp.int32).reshape(-1)
    na = 2 * ntok
    sub = 256
    oh = (ef[:, None] == jnp.arange(N_EXPERTS, dtype=jnp.int32)[None, :]).astype(F32)
    ohb = oh.reshape(na // sub, sub, N_EXPERTS)
    tril = jnp.tril(jnp.ones((sub, sub), F32))
    within = jnp.einsum("ij,bjk->bik", tril, ohb)
    tot = within[:, -1, :]
    before = jnp.cumsum(tot, axis=0) - tot
    rank = jnp.sum((within + before[:, None, :]) * ohb, axis=2).reshape(na) - 1.0
    counts = jnp.sum(tot, axis=0).astype(jnp.int32)
    padded = (counts + MOE_BLK - 1) // MOE_BLK * MOE_BLK
    pad_ends = jnp.cumsum(padded)
    pad_starts = pad_ends - padded
    dest = jnp.sum(oh * pad_starts.astype(F32)[None, :], axis=1) + rank
    dest = dest.astype(jnp.int32)
    n_blocks = na // MOE_BLK + N_EXPERTS
    blk_start = jnp.arange(n_blocks, dtype=jnp.int32) * MOE_BLK
    block_exp = jnp.minimum(jnp.searchsorted(pad_ends, blk_start, side="right"), N_EXPERTS - 1).astype(jnp.int32)
    nused = (pad_ends[-1:] // MOE_BLK).astype(jnp.int32)
    return dest.reshape(ntok // ROW_BLK, 1, 2 * ROW_BLK), block_exp, nused, n_blocks


def _moe(h2, x1, r, modl, lng, lnb, w1, w3, w2, *, with_ctx):
    blk0 = 0 if with_ctx else CTX_LEN // ROW_BLK
    dest3, block_exp, nused, n_blocks = _dispatch_plan(r)
    xs0 = jnp.zeros((n_blocks * MOE_BLK, D_MODEL), F32)
    xs = _dispatch(h2, dest3, xs0)
    y = _experts(block_exp, nused, xs, w1, w3, w2)
    return _combine(dest3, x1, r, modl, lng, lnb, y, blk0=blk0)


def _rope_tables(n):
    tkn = jnp.arange(n, dtype=jnp.int32)
    row = (tkn // GRID_W).astype(F32)
    col = (tkn % GRID_W).astype(F32)
    half = HEAD_DIM // 2
    inv = ROPE_THETA ** (-jnp.arange(0, half, 2, dtype=F32) / half)
    ang_r = row[:, None] * inv[None, :]
    ang_c = col[:, None] * inv[None, :]
    ang = jnp.concatenate([ang_r, ang_r, ang_c, ang_c] * 2, axis=-1)
    cos = jnp.concatenate([jnp.ones((CTX_LEN, LANES), F32), jnp.cos(ang)], axis=0)
    sin = jnp.concatenate([jnp.zeros((CTX_LEN, LANES), F32), jnp.sin(ang)], axis=0)
    return cos, sin


def _kv_block(t):
    for cand in (1280, 1024, 768, 512, 256):
        if t % cand == 0:
            return cand
    raise ValueError(f"unsupported token count {t}")


def kernel(x, c, ctx, c_ctx, w_mod, b_mod, ln_g, ln_b, w_in_even, w_out_even, diff_lam, diff_subln_g,
           gqa_qk_g, w_in_odd, w_out_odd, na_rpb, router_g, router_e, w1, w3, w2):
    b, n, d = x.shape
    assert b == 1 and d == D_MODEL and ctx.shape == (1, CTX_LEN, D_MODEL)
    assert n % (2 * GRID_W) == 0 and n // GRID_W >= BAND_ROWS and (CTX_LEN + n) % ROW_BLK == 0
    t = CTX_LEN + n
    xt = jnp.concatenate([ctx[0], x[0]], axis=0)
    cvec = jnp.zeros((8, d), F32).at[0].set(c[0]).at[1].set(c_ctx)
    mod = _modulation(cvec, w_mod, b_mod)[:, 0:2, :].reshape(DEPTH, 2, 6, d)
    cos, sin = _rope_tables(n)
    hm = jnp.asarray(np.kron(np.eye(2), np.full((HEAD_DIM, HEAD_DIM), 1.0 / HEAD_DIM)), F32)
    rw = jnp.zeros((DEPTH, d, LANES), F32)
    rw = rw.at[:, :, 0:N_GROUPS].set(router_g).at[:, :, N_GROUPS:N_GROUPS + N_EXPERTS].set(router_e)
    bk = _kv_block(t)
    bq = 256
    out = None
    for l in range(DEPTH):
        last = l == DEPTH - 1
        i = l // 2
        modl = mod[l]
        if l % 2 == 0:
            lam_init = 0.8 - 0.6 * math.exp(-0.3 * l)
            q, k, v = _proj_even(xt, modl, w_in_even[i].astype(BF16), cos, sin,
                                 jnp.tile(gqa_qk_g[i], (1, 2)), hm)
            subg = diff_subln_g[i].reshape(1, LANES)
            o_lat = _attn_even(q, k, v, diff_lam[i], subg, q_row0=CTX_LEN, n_q=n, n_k=t, bq=bq, bk=bk,
                               lam_init=lam_init)
            o_ctx = None if last else _attn_even(q, k, v, diff_lam[i], subg, q_row0=0, n_q=CTX_LEN,
                                                 n_k=CTX_LEN, bq=CTX_LEN, bk=CTX_LEN, lam_init=lam_init)
            w_out = w_out_even[i]
        else:
            q, k, v = _proj_odd(xt, modl, w_in_odd[i].astype(BF16))
            bm = _natten_bias_tables(na_rpb[i], n)
            o_lat = _natten(q, k, v, bm, n)
            o_ctx = None if last else _ctx_attn(q, k, v)
            w_out = w_out_odd[i]
        x1, h2, r = _post_attn(o_ctx, o_lat, xt, modl, w_out.astype(BF16), ln_g[l], ln_b[l], rw[l],
                               with_ctx=not last)
        xt = _moe(h2, x1, r, modl, ln_g[l], ln_b[l], w1[l], w3[l], w2[l], with_ctx=not last)
        out = xt
    return out.reshape(1, n, d)
```

```python
import functools
import math

import numpy as np
import jax
import jax.numpy as jnp
from jax import lax
from jax.experimental import pallas as pl
from jax.experimental.pallas import tpu as pltpu

D_MODEL = 1024
DEPTH = 4
GRID_W = 64
CTX_LEN = 256
HEAD_DIM = 64
A_HEADS = 4
B_HEADS = 8
B_KV_HEADS = 2
C_HEADS = D_MODEL // HEAD_DIM
NA_WIN_H = 8
NA_WIN_W = 16
RPB_H = 2 * NA_WIN_H - 1
RPB_W = 2 * NA_WIN_W - 1
N_GROUPS = 4
EXPERTS_PER_GROUP = 8
N_EXPERTS = N_GROUPS * EXPERTS_PER_GROUP
D_EXPERT = 512
ROPE_THETA = 10000.0
DN_ALPHA = (2.0 * DEPTH) ** 0.25
EVEN_IN = 2304
LN_EPS = 1e-6

LANES = 128
ROW_BLK = 256
MOE_BLK = 256
N_MAPS = 16
N_KV_CHUNKS = 5
ATTN_BQ = 512
LOG2E = math.log2(math.e)
BAND_ROWS = 10
NA_QBLK = 2 * GRID_W
NEG = -1e30
VMEM_LIMIT = 48 * 1024 * 1024

F32 = jnp.float32
BF16 = jnp.bfloat16
HIGHEST = lax.Precision.HIGHEST


def _cparams(n_axes):
    return pltpu.CompilerParams(dimension_semantics=("arbitrary",) * n_axes,
                                vmem_limit_bytes=VMEM_LIMIT)


def _dot_nt(a, b):
    return lax.dot_general(a, b, (((1,), (1,)), ((), ())), preferred_element_type=F32)


def _layer_norm(v, g, b):
    mu = jnp.mean(v, axis=-1, keepdims=True)
    d = v - mu
    var = jnp.mean(d * d, axis=-1, keepdims=True)
    return d * lax.rsqrt(var + LN_EPS) * g + b


def _silu(v):
    return v / (1.0 + jnp.exp(-v))


def _mod_kernel(c_ref, w_ref, b_ref, o_ref):
    s = _silu(c_ref[...])
    o_ref[...] = jnp.dot(s, w_ref[...], preferred_element_type=F32, precision=HIGHEST) + b_ref[...]


def _modulation(cvec, w_mod, b_mod):
    d = D_MODEL
    return pl.pallas_call(
        _mod_kernel,
        grid=(DEPTH, 6),
        in_specs=[pl.BlockSpec((8, d), lambda l, j: (0, 0)),
                  pl.BlockSpec((None, d, d), lambda l, j: (l, 0, j)),
                  pl.BlockSpec((None, 1, d), lambda l, j: (l, 0, j))],
        out_specs=pl.BlockSpec((None, 8, d), lambda l, j: (l, 0, j)),
        out_shape=jax.ShapeDtypeStruct((DEPTH, 8, 6 * d), F32),
        compiler_params=_cparams(2),
        name="modulation",
    )(cvec, w_mod, b_mod.reshape(DEPTH, 1, 6 * d))


def _mod_spec(ctx_blk):
    return pl.BlockSpec((None, 6, D_MODEL), lambda i: (jnp.where(i == ctx_blk, 1, 0), 0, 0))


def _rope(z, cos, sin, first16):
    rot = jnp.where(first16, -pltpu.roll(z, LANES - 16, 1), pltpu.roll(z, 16, 1))
    return z * cos + rot * sin


def _proj_even_kernel(x_ref, mod_ref, w_ref, cos_ref, sin_ref, g_ref, hm_ref, qt_ref, k_ref, vt_ref):
    x = x_ref[...]
    h = (x * (1.0 + mod_ref[1:2, :]) + mod_ref[0:1, :]).astype(BF16)
    y = jnp.dot(h, w_ref[...], preferred_element_type=F32)
    cos = cos_ref[...]
    sin = sin_ref[...]
    lane = lax.broadcasted_iota(jnp.int32, cos.shape, 1)
    first16 = (lane % 32) < 16
    lo = lane < HEAD_DIM
    hm = hm_ref[...]
    scale = HEAD_DIM ** -0.5 * LOG2E

    def chunk(c):
        return y[:, c * LANES:(c + 1) * LANES]

    def qk_norm(z, g):
        ms = jnp.dot(z * z, hm, preferred_element_type=F32, precision=HIGHEST)
        return z * lax.rsqrt(ms + LN_EPS) * g

    def put_t(ref, idx, val):
        ref[idx] = val.T.astype(BF16)

    for hd in range(A_HEADS):
        q = _rope(chunk(hd), cos, sin, first16) * scale
        put_t(qt_ref, hd, jnp.where(lo, q, 0.0))
        put_t(qt_ref, A_HEADS + hd, jnp.where(lo, 0.0, q))
        k_ref[hd] = _rope(chunk(4 + hd), cos, sin, first16).astype(BF16)
        put_t(vt_ref, hd, chunk(8 + hd))
    gq = g_ref[0:1, :]
    gk = g_ref[1:2, :]
    for pr in range(B_HEADS // 2):
        q = _rope(qk_norm(chunk(12 + pr), gq), cos, sin, first16) * scale
        qs = pltpu.roll(q, HEAD_DIM, 1)
        if pr < 2:
            put_t(qt_ref, 8 + 2 * pr, jnp.where(lo, q, 0.0))
            put_t(qt_ref, 8 + 2 * pr + 1, jnp.where(lo, qs, 0.0))
        else:
            put_t(qt_ref, 8 + 2 * pr, jnp.where(lo, 0.0, qs))
            put_t(qt_ref, 8 + 2 * pr + 1, jnp.where(lo, 0.0, q))
    k_ref[4] = _rope(qk_norm(chunk(16), gk), cos, sin, first16).astype(BF16)
    put_t(vt_ref, 4, chunk(17))


def _proj_even(xt, modl, w_in, cos, sin, g2, hm):
    t = xt.shape[0]
    nb = t // ROW_BLK
    return pl.pallas_call(
        _proj_even_kernel,
        grid=(nb,),
        in_specs=[pl.BlockSpec((ROW_BLK, D_MODEL), lambda i: (i, 0)),
                  _mod_spec(nb - 1),
                  pl.BlockSpec((D_MODEL, EVEN_IN), lambda i: (0, 0)),
                  pl.BlockSpec((ROW_BLK, LANES), lambda i: (i, 0)),
                  pl.BlockSpec((ROW_BLK, LANES), lambda i: (i, 0)),
                  pl.BlockSpec((2, LANES), lambda i: (0, 0)),
                  pl.BlockSpec((LANES, LANES), lambda i: (0, 0))],
        out_specs=[pl.BlockSpec((N_MAPS, LANES, ROW_BLK), lambda i: (0, 0, i)),
                   pl.BlockSpec((N_KV_CHUNKS, ROW_BLK, LANES), lambda i: (0, i, 0)),
                   pl.BlockSpec((N_KV_CHUNKS, LANES, ROW_BLK), lambda i: (0, 0, i))],
        out_shape=[jax.ShapeDtypeStruct((N_MAPS, LANES, t), BF16),
                   jax.ShapeDtypeStruct((N_KV_CHUNKS, t, LANES), BF16),
                   jax.ShapeDtypeStruct((N_KV_CHUNKS, LANES, t), BF16)],
        compiler_params=_cparams(1),
        name="proj_even",
    )(xt, modl, w_in, cos, sin, g2, hm)


def _kv_chunk(mm):
    return mm % A_HEADS if mm < 2 * A_HEADS else A_HEADS


def _attn_even_kernel(lam_ref, subg_ref, qt_ref, k_ref, vt_ref, o_ref, m_sc, l_sc, acc_sc, *, nkv, lam_init):
    j = pl.program_id(1)

    @pl.when(j == 0)
    def _():
        m_sc[...] = jnp.full(m_sc.shape, NEG, F32)
        l_sc[...] = jnp.zeros(l_sc.shape, F32)
        acc_sc[...] = jnp.zeros(acc_sc.shape, F32)

    def scores(mm):
        return jnp.dot(k_ref[_kv_chunk(mm)], qt_ref[mm], preferred_element_type=F32)

    def accumulate(mm, s):
        m_prev = m_sc[mm]
        m_new = jnp.maximum(m_prev, jnp.max(s, axis=0, keepdims=True))
        alpha = jnp.exp2(m_prev - m_new)
        p = jnp.exp2(s - m_new)
        l_sc[mm] = alpha * l_sc[mm] + jnp.sum(p, axis=0, keepdims=True)
        pv = jnp.dot(vt_ref[_kv_chunk(mm)], p.astype(BF16), preferred_element_type=F32)
        acc_sc[mm] = alpha * acc_sc[mm] + pv
        m_sc[mm] = m_new

    s_next = scores(0)
    for mm in range(N_MAPS):
        s_cur = s_next
        if mm + 1 < N_MAPS:
            s_next = scores(mm + 1)
        accumulate(mm, s_cur)

    @pl.when(j == nkv - 1)
    def _():
        lamv = lam_ref[...]
        e1 = jnp.exp(jnp.sum(lamv[0:1, :] * lamv[1:2, :], axis=1, keepdims=True))
        e2 = jnp.exp(jnp.sum(lamv[2:3, :] * lamv[3:4, :], axis=1, keepdims=True))
        lam = e1 - e2 + lam_init
        subg = subg_ref[...]
        for hd in range(A_HEADS):
            o = acc_sc[hd] / l_sc[hd] - lam * (acc_sc[A_HEADS + hd] / l_sc[A_HEADS + hd])
            ms = jnp.mean(o * o, axis=0, keepdims=True)
            o = o * lax.rsqrt(ms + LN_EPS) * subg * (1.0 - lam_init)
            o_ref[:, hd * LANES:(hd + 1) * LANES] = o.T.astype(BF16)
        for pr in range(B_HEADS // 2):
            oa = acc_sc[8 + 2 * pr] / l_sc[8 + 2 * pr]
            ob = acc_sc[8 + 2 * pr + 1] / l_sc[8 + 2 * pr + 1]
            r0 = 0 if pr < 2 else HEAD_DIM
            o = jnp.concatenate([oa[r0:r0 + HEAD_DIM, :], ob[r0:r0 + HEAD_DIM, :]], axis=0)
            o_ref[:, (A_HEADS + pr) * LANES:(A_HEADS + pr + 1) * LANES] = o.T.astype(BF16)


def _attn_even(qt, k, vt, lamf, subg, *, q_row0, n_q, k_row0, n_k, bq, bk, lam_init):
    assert q_row0 % bq == 0 and n_q % bq == 0 and k_row0 % bk == 0 and n_k % bk == 0
    nq = n_q // bq
    nkv = n_k // bk
    qoff = q_row0 // bq
    koff = k_row0 // bk
    kern = functools.partial(_attn_even_kernel, nkv=nkv, lam_init=lam_init)
    return pl.pallas_call(
        kern,
        grid=(nq, nkv),
        in_specs=[pl.BlockSpec((4, HEAD_DIM), lambda i, j: (0, 0)),
                  pl.BlockSpec((LANES, 1), lambda i, j: (0, 0)),
                  pl.BlockSpec((N_MAPS, LANES, bq), lambda i, j: (0, 0, i + qoff)),
                  pl.BlockSpec((N_KV_CHUNKS, bk, LANES), lambda i, j: (0, j + koff, 0)),
                  pl.BlockSpec((N_KV_CHUNKS, LANES, bk), lambda i, j: (0, 0, j + koff))],
        out_specs=pl.BlockSpec((bq, D_MODEL), lambda i, j: (i, 0)),
        out_shape=jax.ShapeDtypeStruct((n_q, D_MODEL), BF16),
        scratch_shapes=[pltpu.VMEM((N_MAPS, 1, bq), F32), pltpu.VMEM((N_MAPS, 1, bq), F32),
                        pltpu.VMEM((N_MAPS, LANES, bq), F32)],
        compiler_params=_cparams(2),
        name="attn_even",
    )(lamf, subg, qt, k, vt)


def _proj_odd_kernel(x_ref, mod_ref, w_ref, q_ref, k_ref, v_ref):
    x = x_ref[...]
    h = (x * (1.0 + mod_ref[1:2, :]) + mod_ref[0:1, :]).astype(BF16)
    d = D_MODEL
    q_ref[...] = (jnp.dot(h, w_ref[:, 0:d], preferred_element_type=F32) * (HEAD_DIM ** -0.5)).astype(BF16)
    k_ref[...] = jnp.dot(h, w_ref[:, d:2 * d], preferred_element_type=F32).astype(BF16)
    v_ref[...] = jnp.dot(h, w_ref[:, 2 * d:3 * d], preferred_element_type=F32).astype(BF16)


def _proj_odd(xt, modl, w_in):
    t = xt.shape[0]
    d = D_MODEL
    spec = pl.BlockSpec((ROW_BLK, d), lambda i: (i, 0))
    return pl.pallas_call(
        _proj_odd_kernel,
        grid=(t // ROW_BLK,),
        in_specs=[spec, _mod_spec(t // ROW_BLK - 1), pl.BlockSpec((d, 3 * d), lambda i: (0, 0))],
        out_specs=[spec, spec, spec],
        out_shape=[jax.ShapeDtypeStruct((t, d), BF16)] * 3,
        compiler_params=_cparams(1),
        name="proj_odd",
    )(xt, modl, w_in)


def _softmax_av(parts):
    m = None
    for s, _ in parts:
        sm = jnp.max(s, axis=1, keepdims=True)
        m = sm if m is None else jnp.maximum(m, sm)
    l = None
    o = None
    for s, v in parts:
        p = jnp.exp(s - m)
        ps = jnp.sum(p, axis=1, keepdims=True)
        pv = jnp.dot(p.astype(BF16), v, preferred_element_type=F32)
        l = ps if l is None else l + ps
        o = pv if o is None else o + pv
    return o / l


def _natten_kernel(q_ref, k_ref, v_ref, bm_ref, o_ref, *, rows):
    b = pl.program_id(1)
    nb = pl.num_programs(1)
    band = jnp.clip(2 * b - NA_WIN_H // 2, 0, rows - BAND_ROWS)
    typ = jnp.where(b < 2, b, jnp.where(b >= nb - 2, b - (nb - 5), 2))
    start = pl.multiple_of(band * GRID_W, GRID_W)
    nband = BAND_ROWS * GRID_W
    n = rows * GRID_W
    kb = k_ref[pl.ds(start, nband), :]
    vb = v_ref[pl.ds(start, nband), :]
    kc = k_ref[n:n + CTX_LEN, :]
    vc = v_ref[n:n + CTX_LEN, :]
    q = q_ref[...]
    lane = lax.broadcasted_iota(jnp.int32, q.shape, 1)
    lo = lane < HEAD_DIM
    zero = jnp.zeros_like(q)
    outs = []
    for half in range(2):
        qh = jnp.where(lo, q, zero) if half == 0 else jnp.where(lo, zero, q)
        s_nb = _dot_nt(qh, kb) + bm_ref[typ, half]
        s_cx = _dot_nt(qh, kc)
        outs.append(_softmax_av([(s_nb, vb), (s_cx, vc)]))
    o_ref[...] = jnp.where(lo, outs[0], outs[1]).astype(BF16)


def _natten(q, k, v, bm, n):
    t = q.shape[0]
    rows = n // GRID_W
    nqb = n // NA_QBLK
    npair = C_HEADS // 2
    kern = functools.partial(_natten_kernel, rows=rows)
    return pl.pallas_call(
        kern,
        grid=(npair, nqb),
        in_specs=[pl.BlockSpec((NA_QBLK, LANES), lambda j, b: (b, j)),
                  pl.BlockSpec((t, LANES), lambda j, b: (0, j)),
                  pl.BlockSpec((t, LANES), lambda j, b: (0, j)),
                  pl.BlockSpec((5, 2, NA_QBLK, BAND_ROWS * GRID_W), lambda j, b: (0, j, 0, 0))],
        out_specs=pl.BlockSpec((NA_QBLK, LANES), lambda j, b: (b, j)),
        out_shape=jax.ShapeDtypeStruct((n, D_MODEL), BF16),
        compiler_params=_cparams(2),
        name="natten",
    )(q, k, v, bm)


def _ctx_attn_kernel(q_ref, k_ref, v_ref, o_ref):
    q = q_ref[...]
    k = k_ref[...]
    v = v_ref[...]
    lane = lax.broadcasted_iota(jnp.int32, q.shape, 1)
    lo = lane < HEAD_DIM
    zero = jnp.zeros_like(q)
    outs = []
    for half in range(2):
        qh = jnp.where(lo, q, zero) if half == 0 else jnp.where(lo, zero, q)
        outs.append(_softmax_av([(_dot_nt(qh, k), v)]))
    o_ref[...] = jnp.where(lo, outs[0], outs[1]).astype(BF16)


def _ctx_attn(q, k, v):
    ctx_blk = q.shape[0] // CTX_LEN - 1
    spec = pl.BlockSpec((CTX_LEN, LANES), lambda j: (ctx_blk, j))
    return pl.pallas_call(
        _ctx_attn_kernel,
        grid=(C_HEADS // 2,),
        in_specs=[spec, spec, spec],
        out_specs=pl.BlockSpec((CTX_LEN, LANES), lambda j: (0, j)),
        out_shape=jax.ShapeDtypeStruct((CTX_LEN, D_MODEL), BF16),
        compiler_params=_cparams(1),
        name="ctx_attn",
    )(q, k, v)


def _natten_bias_tables(rpb, n):
    rows = n // GRID_W
    nb = rows // 2
    r128 = jnp.zeros((C_HEADS, RPB_H, LANES), F32)
    r128 = r128.at[:, :, 0:NA_WIN_W].set(rpb[:, :, NA_WIN_W - 1:])
    r128 = r128.at[:, :, LANES - (NA_WIN_W - 1):].set(rpb[:, :, 0:NA_WIN_W - 1])
    toep = jnp.tile(r128, (1, 1, GRID_W))[:, :, :GRID_W * (LANES - 1)]
    toep = toep.reshape(C_HEADS, RPB_H, GRID_W, LANES - 1)[:, :, :, :GRID_W]
    c = np.arange(GRID_W)
    cs = np.clip(c - NA_WIN_W // 2, 0, GRID_W - NA_WIN_W)
    kc = np.arange(GRID_W)
    col_ok = (kc[None, :] >= cs[:, None]) & (kc[None, :] < cs[:, None] + NA_WIN_W)
    colmask = jnp.asarray(np.where(col_ok, 0.0, NEG), F32)
    negblk = jnp.full((C_HEADS, GRID_W, GRID_W), NEG, F32)
    tables = []
    for b in (0, 1, 2, nb - 2, nb - 1):
        band = int(np.clip(2 * b - NA_WIN_H // 2, 0, rows - BAND_ROWS))
        qrows = []
        for qr in range(2):
            r = 2 * b + qr
            rs = int(np.clip(r - NA_WIN_H // 2, 0, rows - NA_WIN_H))
            blks = []
            for i in range(BAND_ROWS):
                kr = band + i
                if rs <= kr < rs + NA_WIN_H:
                    blks.append(toep[:, kr - r + NA_WIN_H - 1] + colmask[None])
                else:
                    blks.append(negblk)
            qrows.append(jnp.concatenate(blks, axis=2))
        tables.append(jnp.concatenate(qrows, axis=1))
    return jnp.stack(tables, axis=0)


def _post_attn_kernel(*refs, with_ctx, ctx_blk):
    if with_ctx:
        octx_ref, refs = refs[0], refs[1:]
    olat_ref, x_ref, mod_ref, wout_ref, lng_ref, lnb_ref, rw_ref, x1_ref, h2_ref, r_ref = refs
    o = olat_ref[...]
    if with_ctx:
        o = jnp.where(pl.program_id(0) == ctx_blk, octx_ref[...], o)
    mix = jnp.dot(o, wout_ref[...], preferred_element_type=F32)
    x1 = _layer_norm(DN_ALPHA * x_ref[...] + mod_ref[2:3, :] * mix, lng_ref[0:1, :], lnb_ref[0:1, :])
    x1_ref[...] = x1
    h2 = x1 * (1.0 + mod_ref[4:5, :]) + mod_ref[3:4, :]
    h2_ref[...] = h2
    logits = jnp.dot(h2, rw_ref[...], preferred_element_type=F32, precision=HIGHEST)
    lane = lax.broadcasted_iota(jnp.int32, logits.shape, 1).astype(F32)
    big = float(LANES)
    lg = jnp.where(lane < N_GROUPS, logits, NEG)
    mg = jnp.max(lg, axis=1, keepdims=True)
    gstar = jnp.min(jnp.where(lg == mg, lane, big), axis=1, keepdims=True)
    p_top = 1.0 / jnp.sum(jnp.exp(lg - mg), axis=1, keepdims=True)
    e_lo = N_GROUPS + EXPERTS_PER_GROUP * gstar
    le = jnp.where((lane >= e_lo) & (lane < e_lo + EXPERTS_PER_GROUP), logits, NEG)
    v1 = jnp.max(le, axis=1, keepdims=True)
    i1 = jnp.min(jnp.where(le == v1, lane, big), axis=1, keepdims=True)
    le2 = jnp.where(lane == i1, NEG, le)
    v2 = jnp.max(le2, axis=1, keepdims=True)
    i2 = jnp.min(jnp.where(le2 == v2, lane, big), axis=1, keepdims=True)
    e2 = jnp.exp(v2 - v1)
    w1 = p_top / (1.0 + e2)
    w2 = p_top * e2 / (1.0 + e2)
    r = jnp.where(lane == 0.0, i1 - N_GROUPS, 0.0)
    r = jnp.where(lane == 1.0, i2 - N_GROUPS, r)
    r = jnp.where(lane == 2.0, w1, r)
    r = jnp.where(lane == 3.0, w2, r)
    r_ref[...] = r


def _post_attn(o_ctx, o_lat, xt, modl, w_out, lng, lnb, rw, *, with_ctx):
    t = xt.shape[0]
    d = D_MODEL
    ctx_blk = (t - CTX_LEN) // ROW_BLK
    nblk = ctx_blk + 1 if with_ctx else ctx_blk
    row = pl.BlockSpec((ROW_BLK, d), lambda i: (i, 0))
    in_specs = [pl.BlockSpec((ROW_BLK, d), lambda i: (jnp.minimum(i, ctx_blk - 1), 0)),
                row, _mod_spec(ctx_blk),
                pl.BlockSpec((d, d), lambda i: (0, 0)),
                pl.BlockSpec((2, d), lambda i: (0, 0)),
                pl.BlockSpec((2, d), lambda i: (0, 0)),
                pl.BlockSpec((d, LANES), lambda i: (0, 0))]
    args = [o_lat, xt, modl, w_out, lng, lnb, rw]
    if with_ctx:
        in_specs = [pl.BlockSpec((ROW_BLK, d), lambda i: (0, 0))] + in_specs
        args = [o_ctx] + args
    kern = functools.partial(_post_attn_kernel, with_ctx=with_ctx, ctx_blk=ctx_blk)
    return pl.pallas_call(
        kern,
        grid=(nblk,),
        in_specs=in_specs,
        out_specs=[row, row, pl.BlockSpec((ROW_BLK, LANES), lambda i: (i, 0))],
        out_shape=[jax.ShapeDtypeStruct((nblk * ROW_BLK, d), F32),
                   jax.ShapeDtypeStruct((nblk * ROW_BLK, d), F32),
                   jax.ShapeDtypeStruct((nblk * ROW_BLK, LANES), F32)],
        compiler_params=_cparams(1),
        name="post_attn",
    )(*args)


def _row_copies(n_rows, make):
    def start(r, c):
        for kk in range(2):
            make(r, kk).start()
        return c

    def wait(r, c):
        for kk in range(2):
            make(r, kk).wait()
        return c

    lax.fori_loop(0, n_rows, start, 0)
    lax.fori_loop(0, n_rows, wait, 0)


def _dispatch_kernel(dest_ref, h_ref, xs_in_ref, xs_ref, sem):
    del xs_in_ref

    def make(r, kk):
        dst = dest_ref[0, 0, 2 * r + kk]
        return pltpu.make_async_copy(h_ref.at[pl.ds(r, 1)], xs_ref.at[pl.ds(dst, 1)], sem)

    _row_copies(h_ref.shape[0], make)


def _dispatch(h2, dest3, xs0):
    nblk = dest3.shape[0]
    return pl.pallas_call(
        _dispatch_kernel,
        grid=(nblk,),
        in_specs=[pl.BlockSpec((1, 1, 2 * ROW_BLK), lambda i: (i, 0, 0), memory_space=pltpu.SMEM),
                  pl.BlockSpec((ROW_BLK, D_MODEL), lambda i: (i, 0)),
                  pl.BlockSpec(memory_space=pl.ANY)],
        out_specs=pl.BlockSpec(memory_space=pl.ANY),
        out_shape=jax.ShapeDtypeStruct(xs0.shape, xs0.dtype),
        scratch_shapes=[pltpu.SemaphoreType.DMA(())],
        input_output_aliases={2: 0},
        compiler_params=pltpu.CompilerParams(dimension_semantics=("arbitrary",),
                                             vmem_limit_bytes=VMEM_LIMIT, has_side_effects=True),
        name="moe_dispatch",
    )(dest3, h2, xs0)


def _experts_kernel(bexp_ref, nused_ref, xs_ref, w1_ref, w3_ref, w2_ref, y_ref):
    del bexp_ref
    b = pl.program_id(0)

    @pl.when(b < nused_ref[0])
    def _():
        x = xs_ref[...].astype(BF16)
        h1 = jnp.dot(x, w1_ref[...].astype(BF16), preferred_element_type=F32)
        h3 = jnp.dot(x, w3_ref[...].astype(BF16), preferred_element_type=F32)
        a = (_silu(h1) * h3).astype(BF16)
        y_ref[...] = jnp.dot(a, w2_ref[...].astype(BF16), preferred_element_type=F32)

    @pl.when(b >= nused_ref[0])
    def _():
        y_ref[...] = jnp.zeros(y_ref.shape, F32)


def _experts(block_exp, nused, xs, w1, w3, w2):
    n_rows = xs.shape[0]
    d = D_MODEL
    grid_spec = pltpu.PrefetchScalarGridSpec(
        num_scalar_prefetch=2,
        grid=(n_rows // MOE_BLK,),
        in_specs=[pl.BlockSpec((MOE_BLK, d), lambda b, be, nu: (jnp.minimum(b, nu[0] - 1), 0)),
                  pl.BlockSpec((None, d, D_EXPERT), lambda b, be, nu: (be[b], 0, 0)),
                  pl.BlockSpec((None, d, D_EXPERT), lambda b, be, nu: (be[b], 0, 0)),
                  pl.BlockSpec((None, D_EXPERT, d), lambda b, be, nu: (be[b], 0, 0))],
        out_specs=pl.BlockSpec((MOE_BLK, d), lambda b, be, nu: (b, 0)),
    )
    return pl.pallas_call(
        _experts_kernel,
        grid_spec=grid_spec,
        out_shape=jax.ShapeDtypeStruct((n_rows, d), F32),
        compiler_params=_cparams(1),
        name="moe_experts",
    )(block_exp, nused, xs, w1, w3, w2)


def _combine_kernel(dest_ref, x1_ref, r_ref, mod_ref, lng_ref, lnb_ref, y_ref, o_ref, ybuf, sem):
    def make(r, kk):
        src = dest_ref[0, 0, 2 * r + kk]
        return pltpu.make_async_copy(y_ref.at[pl.ds(src, 1)], ybuf.at[kk, pl.ds(r, 1)], sem)

    _row_copies(x1_ref.shape[0], make)
    ymix = ybuf[0] * r_ref[:, 2:3] + ybuf[1] * r_ref[:, 3:4]
    o_ref[...] = _layer_norm(DN_ALPHA * x1_ref[...] + mod_ref[5:6, :] * ymix, lng_ref[1:2, :], lnb_ref[1:2, :])


def _combine(dest3, x1, r, modl, lng, lnb, y, *, ctx_blk):
    nblk = dest3.shape[0]
    d = D_MODEL
    return pl.pallas_call(
        _combine_kernel,
        grid=(nblk,),
        in_specs=[pl.BlockSpec((1, 1, 2 * ROW_BLK), lambda i: (i, 0, 0), memory_space=pltpu.SMEM),
                  pl.BlockSpec((ROW_BLK, d), lambda i: (i, 0)),
                  pl.BlockSpec((ROW_BLK, LANES), lambda i: (i, 0)),
                  _mod_spec(ctx_blk),
                  pl.BlockSpec((2, d), lambda i: (0, 0)),
                  pl.BlockSpec((2, d), lambda i: (0, 0)),
                  pl.BlockSpec(memory_space=pl.ANY)],
        out_specs=pl.BlockSpec((ROW_BLK, d), lambda i: (i, 0)),
        out_shape=jax.ShapeDtypeStruct((nblk * ROW_BLK, d), F32),
        scratch_shapes=[pltpu.VMEM((2, ROW_BLK, d), F32), pltpu.SemaphoreType.DMA(())],
        compiler_params=_cparams(1),
        name="moe_combine",
    )(dest3, x1, r, modl, lng, lnb, y)


def _dispatch_plan(r):
    ntok = r.shape[0]
    ef = r[:, 0:2].astype(jnp.int32).reshape(-1)
    na = 2 * ntok
    sub = 256
    oh = (ef[:, None] == jnp.arange(N_EXPERTS, dtype=jnp.int32)[None, :]).astype(F32)
    ohb = oh.reshape(na // sub, sub, N_EXPERTS)
    tril = jnp.tril(jnp.ones((sub, sub), F32))
    within = jnp.einsum("ij,bjk->bik", tril, ohb)
    tot = within[:, -1, :]
    before = jnp.cumsum(tot, axis=0) - tot
    rank = jnp.sum((within + before[:, None, :]) * ohb, axis=2).reshape(na) - 1.0
    counts = jnp.sum(tot, axis=0).astype(jnp.int32)
    padded = (counts + MOE_BLK - 1) // MOE_BLK * MOE_BLK
    pad_ends = jnp.cumsum(padded)
    pad_starts = pad_ends - padded
    dest = jnp.sum(oh * pad_starts.astype(F32)[None, :], axis=1) + rank
    dest = dest.astype(jnp.int32)
    n_blocks = na // MOE_BLK + N_EXPERTS
    blk_start = jnp.arange(n_blocks, dtype=jnp.int32) * MOE_BLK
    block_exp = jnp.sum((pad_ends[None, :] <= blk_start[:, None]).astype(jnp.int32), axis=1)
    block_exp = jnp.minimum(block_exp, N_EXPERTS - 1)
    nused = (pad_ends[-1:] // MOE_BLK).astype(jnp.int32)
    return dest.reshape(ntok // ROW_BLK, 1, 2 * ROW_BLK), block_exp, nused, n_blocks


def _moe(h2, x1, r, modl, lng, lnb, w1, w3, w2, *, ctx_blk):
    dest3, block_exp, nused, n_blocks = _dispatch_plan(r)
    xs0 = jnp.zeros((n_blocks * MOE_BLK, D_MODEL), F32)
    xs = _dispatch(h2, dest3, xs0)
    y = _experts(block_exp, nused, xs, w1, w3, w2)
    return _combine(dest3, x1, r, modl, lng, lnb, y, ctx_blk=ctx_blk)


def _rope_tables(n):
    tkn = jnp.arange(n, dtype=jnp.int32)
    row = (tkn // GRID_W).astype(F32)
    col = (tkn % GRID_W).astype(F32)
    half = HEAD_DIM // 2
    inv = ROPE_THETA ** (-jnp.arange(0, half, 2, dtype=F32) / half)
    ang_r = row[:, None] * inv[None, :]
    ang_c = col[:, None] * inv[None, :]
    ang = jnp.concatenate([ang_r, ang_r, ang_c, ang_c] * 2, axis=-1)
    cos = jnp.concatenate([jnp.cos(ang), jnp.ones((CTX_LEN, LANES), F32)], axis=0)
    sin = jnp.concatenate([jnp.sin(ang), jnp.zeros((CTX_LEN, LANES), F32)], axis=0)
    return cos, sin


def _kv_block(t):
    for cand in (1280, 1024, 768, 512, 256):
        if t % cand == 0:
            return cand
    raise ValueError(f"unsupported token count {t}")


def kernel(x, c, ctx, c_ctx, w_mod, b_mod, ln_g, ln_b, w_in_even, w_out_even, diff_lam, diff_subln_g,
           gqa_qk_g, w_in_odd, w_out_odd, na_rpb, router_g, router_e, w1, w3, w2):
    b, n, d = x.shape
    assert b == 1 and d == D_MODEL and ctx.shape == (1, CTX_LEN, D_MODEL)
    assert n % ROW_BLK == 0 and n // GRID_W >= BAND_ROWS and CTX_LEN == ROW_BLK
    t = n + CTX_LEN
    ctx_blk = n // ROW_BLK
    xt = jnp.concatenate([x[0], ctx[0]], axis=0)
    cvec = jnp.zeros((8, d), F32).at[0].set(c[0]).at[1].set(c_ctx)
    mod = _modulation(cvec, w_mod, b_mod)[:, 0:2, :].reshape(DEPTH, 2, 6, d)
    cos, sin = _rope_tables(n)
    hm = jnp.asarray(np.kron(np.eye(2), np.full((HEAD_DIM, HEAD_DIM), 1.0 / HEAD_DIM)), F32)
    rw = jnp.zeros((DEPTH, d, LANES), F32)
    rw = rw.at[:, :, 0:N_GROUPS].set(router_g).at[:, :, N_GROUPS:N_GROUPS + N_EXPERTS].set(router_e)
    bk = _kv_block(t)
    bq = ATTN_BQ if n % ATTN_BQ == 0 else ROW_BLK
    out = None
    for l in range(DEPTH):
        last = l == DEPTH - 1
        i = l // 2
        modl = mod[l]
        if l % 2 == 0:
            lam_init = 0.8 - 0.6 * math.exp(-0.3 * l)
            q, k, v = _proj_even(xt, modl, w_in_even[i].astype(BF16), cos, sin,
                                 jnp.tile(gqa_qk_g[i], (1, 2)), hm)
            subg = diff_subln_g[i].reshape(LANES, 1)
            o_lat = _attn_even(q, k, v, diff_lam[i], subg, q_row0=0, n_q=n, k_row0=0, n_k=t, bq=bq, bk=bk,
                               lam_init=lam_init)
            o_ctx = None if last else _attn_even(q, k, v, diff_lam[i], subg, q_row0=n, n_q=CTX_LEN, k_row0=n,
                                                 n_k=CTX_LEN, bq=CTX_LEN, bk=CTX_LEN, lam_init=lam_init)
            w_out = w_out_even[i]
        else:
            q, k, v = _proj_odd(xt, modl, w_in_odd[i].astype(BF16))
            bm = _natten_bias_tables(na_rpb[i], n)
            o_lat = _natten(q, k, v, bm, n)
            o_ctx = None if last else _ctx_attn(q, k, v)
            w_out = w_out_odd[i]
        x1, h2, r = _post_attn(o_ctx, o_lat, xt, modl, w_out.astype(BF16), ln_g[l], ln_b[l], rw[l],
                               with_ctx=not last)
        xt = _moe(h2, x1, r, modl, ln_g[l], ln_b[l], w1[l], w3[l], w2[l], ctx_blk=ctx_blk)
        out = xt
    return out.reshape(1, n, d)
```

```python
import functools
import math

import numpy as np
import jax
import jax.numpy as jnp
from jax import lax
from jax.experimental import pallas as pl
from jax.experimental.pallas import tpu as pltpu

D_MODEL = 1024
DEPTH = 4
GRID_W = 64
CTX_LEN = 256
HEAD_DIM = 64
A_HEADS = 4
B_HEADS = 8
B_KV_HEADS = 2
C_HEADS = D_MODEL // HEAD_DIM
NA_WIN_H = 8
NA_WIN_W = 16
RPB_H = 2 * NA_WIN_H - 1
RPB_W = 2 * NA_WIN_W - 1
N_GROUPS = 4
EXPERTS_PER_GROUP = 8
N_EXPERTS = N_GROUPS * EXPERTS_PER_GROUP
D_EXPERT = 512
ROPE_THETA = 10000.0
DN_ALPHA = (2.0 * DEPTH) ** 0.25
EVEN_IN = 2304
LN_EPS = 1e-6

LANES = 128
ROW_BLK = 256
MOE_BLK = 256
N_MAPS = 16
N_KV_CHUNKS = 5
ATTN_BQ = 512
LOG2E = math.log2(math.e)
SAFE_LOG2_GAP = 80.0
NORM_SLACK = 1.001
BAND_ROWS = 10
NA_QBLK = 2 * GRID_W
NEG = -1e30
VMEM_LIMIT = 48 * 1024 * 1024

F32 = jnp.float32
BF16 = jnp.bfloat16
HIGHEST = lax.Precision.HIGHEST


def _cparams(n_axes):
    return pltpu.CompilerParams(dimension_semantics=("arbitrary",) * n_axes,
                                vmem_limit_bytes=VMEM_LIMIT)


def _dot_nt(a, b):
    return lax.dot_general(a, b, (((1,), (1,)), ((), ())), preferred_element_type=F32)


def _layer_norm(v, g, b):
    mu = jnp.mean(v, axis=-1, keepdims=True)
    d = v - mu
    var = jnp.mean(d * d, axis=-1, keepdims=True)
    return d * lax.rsqrt(var + LN_EPS) * g + b


def _silu(v):
    return v / (1.0 + jnp.exp(-v))


def _mod_kernel(c_ref, w_ref, b_ref, o_ref):
    s = _silu(c_ref[...])
    o_ref[...] = jnp.dot(s, w_ref[...], preferred_element_type=F32, precision=HIGHEST) + b_ref[...]


def _modulation(cvec, w_mod, b_mod):
    d = D_MODEL
    return pl.pallas_call(
        _mod_kernel,
        grid=(DEPTH, 6),
        in_specs=[pl.BlockSpec((8, d), lambda l, j: (0, 0)),
                  pl.BlockSpec((None, d, d), lambda l, j: (l, 0, j)),
                  pl.BlockSpec((None, 1, d), lambda l, j: (l, 0, j))],
        out_specs=pl.BlockSpec((None, 8, d), lambda l, j: (l, 0, j)),
        out_shape=jax.ShapeDtypeStruct((DEPTH, 8, 6 * d), F32),
        compiler_params=_cparams(2),
        name="modulation",
    )(cvec, w_mod, b_mod.reshape(DEPTH, 1, 6 * d))


def _mod_spec(ctx_blk):
    return pl.BlockSpec((None, 6, D_MODEL), lambda i: (jnp.where(i == ctx_blk, 1, 0), 0, 0))


def _rope(z, cos, sin, first16):
    rot = jnp.where(first16, -pltpu.roll(z, LANES - 16, 1), pltpu.roll(z, 16, 1))
    return z * cos + rot * sin


def _proj_even_kernel(x_ref, mod_ref, w_ref, cos_ref, sin_ref, g_ref, hm_ref, qt_ref, k_ref, vt_ref, qn_ref, kmx_ref):
    x = x_ref[...]
    h = (x * (1.0 + mod_ref[1:2, :]) + mod_ref[0:1, :]).astype(BF16)
    y = jnp.dot(h, w_ref[...], preferred_element_type=F32)
    cos = cos_ref[...]
    sin = sin_ref[...]
    lane = lax.broadcasted_iota(jnp.int32, cos.shape, 1)
    first16 = (lane % 32) < 16
    lo = lane < HEAD_DIM
    hm = hm_ref[...]
    scale = HEAD_DIM ** -0.5 * LOG2E

    def chunk(c):
        return y[:, c * LANES:(c + 1) * LANES]

    def qk_norm(z, g):
        ms = jnp.dot(z * z, hm, preferred_element_type=F32, precision=HIGHEST)
        return z * lax.rsqrt(ms + LN_EPS) * g

    def put_t(ref, idx, val):
        ref[idx] = val.T.astype(BF16)

    def put_q(idx, val):
        vb = val.T.astype(BF16)
        qt_ref[idx] = vb
        vf = vb.astype(F32)
        qn_ref[idx:idx + 1, :] = jnp.sqrt(jnp.sum(vf * vf, axis=0, keepdims=True))

    kmax = jnp.zeros(kmx_ref.shape, F32)
    klane = lax.broadcasted_iota(jnp.int32, kmx_ref.shape, 1)

    def put_k(idx, val, kmax):
        kb = val.astype(BF16)
        k_ref[idx] = kb
        kf = kb.astype(F32)
        n2 = jnp.max(jnp.sum(kf * kf, axis=1, keepdims=True), axis=0, keepdims=True)
        return jnp.where(klane == idx, n2, kmax)

    for hd in range(A_HEADS):
        q = _rope(chunk(hd), cos, sin, first16) * scale
        put_q(hd, jnp.where(lo, q, 0.0))
        put_q(A_HEADS + hd, jnp.where(lo, 0.0, q))
        kmax = put_k(hd, _rope(chunk(4 + hd), cos, sin, first16), kmax)
        put_t(vt_ref, hd, chunk(8 + hd))
    gq = g_ref[0:1, :]
    gk = g_ref[1:2, :]
    for pr in range(B_HEADS // 2):
        q = _rope(qk_norm(chunk(12 + pr), gq), cos, sin, first16) * scale
        qs = pltpu.roll(q, HEAD_DIM, 1)
        if pr < 2:
            put_q(8 + 2 * pr, jnp.where(lo, q, 0.0))
            put_q(8 + 2 * pr + 1, jnp.where(lo, qs, 0.0))
        else:
            put_q(8 + 2 * pr, jnp.where(lo, 0.0, qs))
            put_q(8 + 2 * pr + 1, jnp.where(lo, 0.0, q))
    kmax = put_k(4, _rope(qk_norm(chunk(16), gk), cos, sin, first16), kmax)
    put_t(vt_ref, 4, chunk(17))
    kmx_ref[...] = kmax


def _proj_even(xt, modl, w_in, cos, sin, g2, hm):
    t = xt.shape[0]
    nb = t // ROW_BLK
    return pl.pallas_call(
        _proj_even_kernel,
        grid=(nb,),
        in_specs=[pl.BlockSpec((ROW_BLK, D_MODEL), lambda i: (i, 0)),
                  _mod_spec(nb - 1),
                  pl.BlockSpec((D_MODEL, EVEN_IN), lambda i: (0, 0)),
                  pl.BlockSpec((ROW_BLK, LANES), lambda i: (i, 0)),
                  pl.BlockSpec((ROW_BLK, LANES), lambda i: (i, 0)),
                  pl.BlockSpec((2, LANES), lambda i: (0, 0)),
                  pl.BlockSpec((LANES, LANES), lambda i: (0, 0))],
        out_specs=[pl.BlockSpec((N_MAPS, LANES, ROW_BLK), lambda i: (0, 0, i)),
                   pl.BlockSpec((N_KV_CHUNKS, ROW_BLK, LANES), lambda i: (0, i, 0)),
                   pl.BlockSpec((N_KV_CHUNKS, LANES, ROW_BLK), lambda i: (0, 0, i)),
                   pl.BlockSpec((N_MAPS, ROW_BLK), lambda i: (0, i)),
                   pl.BlockSpec((None, 8, LANES), lambda i: (i, 0, 0))],
        out_shape=[jax.ShapeDtypeStruct((N_MAPS, LANES, t), BF16),
                   jax.ShapeDtypeStruct((N_KV_CHUNKS, t, LANES), BF16),
                   jax.ShapeDtypeStruct((N_KV_CHUNKS, LANES, t), BF16),
                   jax.ShapeDtypeStruct((N_MAPS, t), F32),
                   jax.ShapeDtypeStruct((nb, 8, LANES), F32)],
        compiler_params=_cparams(1),
        name="proj_even",
    )(xt, modl, w_in, cos, sin, g2, hm)


def _kv_chunk(mm):
    return mm % A_HEADS if mm < 2 * A_HEADS else A_HEADS


def _attn_even_kernel(kmax_ref, lam_ref, subg_ref, qn_ref, qt_ref, k_ref, vt_ref, o_ref, m_sc, l_sc, acc_sc, *,
                      nkv, lam_init):
    j = pl.program_id(1)

    @pl.when(j == 0)
    def _():
        m_sc[...] = jnp.full(m_sc.shape, NEG, F32)
        l_sc[...] = jnp.zeros(l_sc.shape, F32)
        acc_sc[...] = jnp.zeros(acc_sc.shape, F32)

    gap = None
    for mm in range(N_MAPS):
        ub = qn_ref[mm:mm + 1, :] * (kmax_ref[j, _kv_chunk(mm)] * NORM_SLACK)
        g = ub - m_sc[mm]
        gap = g if gap is None else jnp.maximum(gap, g)
    keep_ref = jnp.max(gap) < SAFE_LOG2_GAP

    def scores(mm):
        return jnp.dot(k_ref[_kv_chunk(mm)], qt_ref[mm], preferred_element_type=F32)

    def accumulate_keep(mm, s):
        p = jnp.exp2(s - m_sc[mm])
        l_sc[mm] = l_sc[mm] + jnp.sum(p, axis=0, keepdims=True)
        acc_sc[mm] = acc_sc[mm] + jnp.dot(vt_ref[_kv_chunk(mm)], p.astype(BF16), preferred_element_type=F32)

    def accumulate_move(mm, s):
        m_prev = m_sc[mm]
        m_new = jnp.maximum(m_prev, jnp.max(s, axis=0, keepdims=True))
        alpha = jnp.exp2(m_prev - m_new)
        p = jnp.exp2(s - m_new)
        l_sc[mm] = alpha * l_sc[mm] + jnp.sum(p, axis=0, keepdims=True)
        pv = jnp.dot(vt_ref[_kv_chunk(mm)], p.astype(BF16), preferred_element_type=F32)
        acc_sc[mm] = alpha * acc_sc[mm] + pv
        m_sc[mm] = m_new

    def all_maps(accumulate):
        s_next = scores(0)
        for mm in range(N_MAPS):
            s_cur = s_next
            if mm + 1 < N_MAPS:
                s_next = scores(mm + 1)
            accumulate(mm, s_cur)

    @pl.when(keep_ref)
    def _():
        all_maps(accumulate_keep)

    @pl.when(jnp.logical_not(keep_ref))
    def _():
        all_maps(accumulate_move)

    @pl.when(j == nkv - 1)
    def _():
        lamv = lam_ref[...]
        e1 = jnp.exp(jnp.sum(lamv[0:1, :] * lamv[1:2, :], axis=1, keepdims=True))
        e2 = jnp.exp(jnp.sum(lamv[2:3, :] * lamv[3:4, :], axis=1, keepdims=True))
        lam = e1 - e2 + lam_init
        subg = subg_ref[...]
        for hd in range(A_HEADS):
            o = acc_sc[hd] / l_sc[hd] - lam * (acc_sc[A_HEADS + hd] / l_sc[A_HEADS + hd])
            ms = jnp.mean(o * o, axis=0, keepdims=True)
            o = o * lax.rsqrt(ms + LN_EPS) * subg * (1.0 - lam_init)
            o_ref[:, hd * LANES:(hd + 1) * LANES] = o.T.astype(BF16)
        for pr in range(B_HEADS // 2):
            oa = acc_sc[8 + 2 * pr] / l_sc[8 + 2 * pr]
            ob = acc_sc[8 + 2 * pr + 1] / l_sc[8 + 2 * pr + 1]
            r0 = 0 if pr < 2 else HEAD_DIM
            o = jnp.concatenate([oa[r0:r0 + HEAD_DIM, :], ob[r0:r0 + HEAD_DIM, :]], axis=0)
            o_ref[:, (A_HEADS + pr) * LANES:(A_HEADS + pr + 1) * LANES] = o.T.astype(BF16)


def _attn_even(qt, k, vt, qn, kmx, lamf, subg, *, q_row0, n_q, k_row0, n_k, bq, bk, lam_init):
    assert q_row0 % bq == 0 and n_q % bq == 0 and k_row0 % bk == 0 and n_k % bk == 0 and bk % ROW_BLK == 0
    nq = n_q // bq
    nkv = n_k // bk
    qoff = q_row0 // bq
    koff = k_row0 // bk
    kmax = kmx[k_row0 // ROW_BLK:(k_row0 + n_k) // ROW_BLK, 0, 0:8]
    kmax = jnp.sqrt(jnp.max(kmax.reshape(nkv, bk // ROW_BLK, 8), axis=1))
    kern = functools.partial(_attn_even_kernel, nkv=nkv, lam_init=lam_init)
    return pl.pallas_call(
        kern,
        grid=(nq, nkv),
        in_specs=[pl.BlockSpec(memory_space=pltpu.SMEM),
                  pl.BlockSpec((4, HEAD_DIM), lambda i, j: (0, 0)),
                  pl.BlockSpec((LANES, 1), lambda i, j: (0, 0)),
                  pl.BlockSpec((N_MAPS, bq), lambda i, j: (0, i + qoff)),
                  pl.BlockSpec((N_MAPS, LANES, bq), lambda i, j: (0, 0, i + qoff)),
                  pl.BlockSpec((N_KV_CHUNKS, bk, LANES), lambda i, j: (0, j + koff, 0)),
                  pl.BlockSpec((N_KV_CHUNKS, LANES, bk), lambda i, j: (0, 0, j + koff))],
        out_specs=pl.BlockSpec((bq, D_MODEL), lambda i, j: (i, 0)),
        out_shape=jax.ShapeDtypeStruct((n_q, D_MODEL), BF16),
        scratch_shapes=[pltpu.VMEM((N_MAPS, 1, bq), F32), pltpu.VMEM((N_MAPS, 1, bq), F32),
                        pltpu.VMEM((N_MAPS, LANES, bq), F32)],
        compiler_params=_cparams(2),
        name="attn_even",
    )(kmax, lamf, subg, qn, qt, k, vt)


def _proj_odd_kernel(x_ref, mod_ref, w_ref, q_ref, k_ref, v_ref):
    x = x_ref[...]
    h = (x * (1.0 + mod_ref[1:2, :]) + mod_ref[0:1, :]).astype(BF16)
    d = D_MODEL
    q_ref[...] = (jnp.dot(h, w_ref[:, 0:d], preferred_element_type=F32) * (HEAD_DIM ** -0.5)).astype(BF16)
    k_ref[...] = jnp.dot(h, w_ref[:, d:2 * d], preferred_element_type=F32).astype(BF16)
    v_ref[...] = jnp.dot(h, w_ref[:, 2 * d:3 * d], preferred_element_type=F32).astype(BF16)


def _proj_odd(xt, modl, w_in):
    t = xt.shape[0]
    d = D_MODEL
    spec = pl.BlockSpec((ROW_BLK, d), lambda i: (i, 0))
    return pl.pallas_call(
        _proj_odd_kernel,
        grid=(t // ROW_BLK,),
        in_specs=[spec, _mod_spec(t // ROW_BLK - 1), pl.BlockSpec((d, 3 * d), lambda i: (0, 0))],
        out_specs=[spec, spec, spec],
        out_shape=[jax.ShapeDtypeStruct((t, d), BF16)] * 3,
        compiler_params=_cparams(1),
        name="proj_odd",
    )(xt, modl, w_in)


def _softmax_av(parts):
    m = None
    for s, _ in parts:
        sm = jnp.max(s, axis=1, keepdims=True)
        m = sm if m is None else jnp.maximum(m, sm)
    l = None
    o = None
    for s, v in parts:
        p = jnp.exp(s - m)
        ps = jnp.sum(p, axis=1, keepdims=True)
        pv = jnp.dot(p.astype(BF16), v, preferred_element_type=F32)
        l = ps if l is None else l + ps
        o = pv if o is None else o + pv
    return o / l


def _natten_kernel(q_ref, k_ref, v_ref, bm_ref, o_ref, *, rows):
    b = pl.program_id(1)
    nb = pl.num_programs(1)
    band = jnp.clip(2 * b - NA_WIN_H // 2, 0, rows - BAND_ROWS)
    typ = jnp.where(b < 2, b, jnp.where(b >= nb - 2, b - (nb - 5), 2))
    start = pl.multiple_of(band * GRID_W, GRID_W)
    nband = BAND_ROWS * GRID_W
    n = rows * GRID_W
    kb = k_ref[pl.ds(start, nband), :]
    vb = v_ref[pl.ds(start, nband), :]
    kc = k_ref[n:n + CTX_LEN, :]
    vc = v_ref[n:n + CTX_LEN, :]
    q = q_ref[...]
    lane = lax.broadcasted_iota(jnp.int32, q.shape, 1)
    lo = lane < HEAD_DIM
    zero = jnp.zeros_like(q)
    scores = []
    for half in range(2):
        qh = jnp.where(lo, q, zero) if half == 0 else jnp.where(lo, zero, q)
        scores.append((_dot_nt(qh, kb), _dot_nt(qh, kc)))
    outs = []
    for half in range(2):
        s_nb, s_cx = scores[half]
        outs.append(_softmax_av([(s_nb + bm_ref[typ, half], vb), (s_cx, vc)]))
    o_ref[...] = jnp.where(lo, outs[0], outs[1]).astype(BF16)


def _natten(q, k, v, bm, n):
    t = q.shape[0]
    rows = n // GRID_W
    nqb = n // NA_QBLK
    npair = C_HEADS // 2
    kern = functools.partial(_natten_kernel, rows=rows)
    return pl.pallas_call(
        kern,
        grid=(npair, nqb),
        in_specs=[pl.BlockSpec((NA_QBLK, LANES), lambda j, b: (b, j)),
                  pl.BlockSpec((t, LANES), lambda j, b: (0, j)),
                  pl.BlockSpec((t, LANES), lambda j, b: (0, j)),
                  pl.BlockSpec((5, 2, NA_QBLK, BAND_ROWS * GRID_W), lambda j, b: (0, j, 0, 0))],
        out_specs=pl.BlockSpec((NA_QBLK, LANES), lambda j, b: (b, j)),
        out_shape=jax.ShapeDtypeStruct((n, D_MODEL), BF16),
        compiler_params=_cparams(2),
        name="natten",
    )(q, k, v, bm)


def _ctx_attn_kernel(q_ref, k_ref, v_ref, o_ref):
    q = q_ref[...]
    k = k_ref[...]
    v = v_ref[...]
    lane = lax.broadcasted_iota(jnp.int32, q.shape, 1)
    lo = lane < HEAD_DIM
    zero = jnp.zeros_like(q)
    outs = []
    for half in range(2):
        qh = jnp.where(lo, q, zero) if half == 0 else jnp.where(lo, zero, q)
        outs.append(_softmax_av([(_dot_nt(qh, k), v)]))
    o_ref[...] = jnp.where(lo, outs[0], outs[1]).astype(BF16)


def _ctx_attn(q, k, v):
    ctx_blk = q.shape[0] // CTX_LEN - 1
    spec = pl.BlockSpec((CTX_LEN, LANES), lambda j: (ctx_blk, j))
    return pl.pallas_call(
        _ctx_attn_kernel,
        grid=(C_HEADS // 2,),
        in_specs=[spec, spec, spec],
        out_specs=pl.BlockSpec((CTX_LEN, LANES), lambda j: (0, j)),
        out_shape=jax.ShapeDtypeStruct((CTX_LEN, D_MODEL), BF16),
        compiler_params=_cparams(1),
        name="ctx_attn",
    )(q, k, v)


def _natten_bias_tables(rpb, n):
    rows = n // GRID_W
    nb = rows // 2
    c = np.arange(GRID_W)
    sel = (np.arange(RPB_W)[:, None, None] == (c[None, None, :] - c[None, :, None] + NA_WIN_W - 1))
    toep = jnp.einsum("hdm,mck->hdck", rpb, jnp.asarray(sel, F32), precision=HIGHEST)
    cs = np.clip(c - NA_WIN_W // 2, 0, GRID_W - NA_WIN_W)
    kc = np.arange(GRID_W)
    col_ok = (kc[None, :] >= cs[:, None]) & (kc[None, :] < cs[:, None] + NA_WIN_W)
    colmask = jnp.asarray(np.where(col_ok, 0.0, NEG), F32)
    negblk = jnp.full((C_HEADS, GRID_W, GRID_W), NEG, F32)
    tables = []
    for b in (0, 1, 2, nb - 2, nb - 1):
        band = int(np.clip(2 * b - NA_WIN_H // 2, 0, rows - BAND_ROWS))
        qrows = []
        for qr in range(2):
            r = 2 * b + qr
            rs = int(np.clip(r - NA_WIN_H // 2, 0, rows - NA_WIN_H))
            blks = []
            for i in range(BAND_ROWS):
                kr = band + i
                if rs <= kr < rs + NA_WIN_H:
                    blks.append(toep[:, kr - r + NA_WIN_H - 1] + colmask[None])
                else:
                    blks.append(negblk)
            qrows.append(jnp.concatenate(blks, axis=2))
        tables.append(jnp.concatenate(qrows, axis=1))
    return jnp.stack(tables, axis=0)


def _post_attn_kernel(*refs, with_ctx, ctx_blk):
    if with_ctx:
        octx_ref, refs = refs[0], refs[1:]
    olat_ref, x_ref, mod_ref, wout_ref, lng_ref, lnb_ref, rw_ref, x1_ref, h2_ref, r_ref = refs
    o = olat_ref[...]
    if with_ctx:
        o = jnp.where(pl.program_id(0) == ctx_blk, octx_ref[...], o)
    mix = jnp.dot(o, wout_ref[...], preferred_element_type=F32)
    x1 = _layer_norm(DN_ALPHA * x_ref[...] + mod_ref[2:3, :] * mix, lng_ref[0:1, :], lnb_ref[0:1, :])
    x1_ref[...] = x1
    h2 = x1 * (1.0 + mod_ref[4:5, :]) + mod_ref[3:4, :]
    h2_ref[...] = h2
    logits = jnp.dot(h2, rw_ref[...], preferred_element_type=F32, precision=HIGHEST)
    lane = lax.broadcasted_iota(jnp.int32, logits.shape, 1).astype(F32)
    big = float(LANES)
    lg = jnp.where(lane < N_GROUPS, logits, NEG)
    mg = jnp.max(lg, axis=1, keepdims=True)
    gstar = jnp.min(jnp.where(lg == mg, lane, big), axis=1, keepdims=True)
    p_top = 1.0 / jnp.sum(jnp.exp(lg - mg), axis=1, keepdims=True)
    e_lo = N_GROUPS + EXPERTS_PER_GROUP * gstar
    le = jnp.where((lane >= e_lo) & (lane < e_lo + EXPERTS_PER_GROUP), logits, NEG)
    v1 = jnp.max(le, axis=1, keepdims=True)
    i1 = jnp.min(jnp.where(le == v1, lane, big), axis=1, keepdims=True)
    le2 = jnp.where(lane == i1, NEG, le)
    v2 = jnp.max(le2, axis=1, keepdims=True)
    i2 = jnp.min(jnp.where(le2 == v2, lane, big), axis=1, keepdims=True)
    e2 = jnp.exp(v2 - v1)
    w1 = p_top / (1.0 + e2)
    w2 = p_top * e2 / (1.0 + e2)
    r = jnp.where(lane == 0.0, i1 - N_GROUPS, 0.0)
    r = jnp.where(lane == 1.0, i2 - N_GROUPS, r)
    r = jnp.where(lane == 2.0, w1, r)
    r = jnp.where(lane == 3.0, w2, r)
    r_ref[...] = r


def _post_attn(o_ctx, o_lat, xt, modl, w_out, lng, lnb, rw, *, with_ctx):
    t = xt.shape[0]
    d = D_MODEL
    ctx_blk = (t - CTX_LEN) // ROW_BLK
    nblk = ctx_blk + 1 if with_ctx else ctx_blk
    row = pl.BlockSpec((ROW_BLK, d), lambda i: (i, 0))
    in_specs = [pl.BlockSpec((ROW_BLK, d), lambda i: (jnp.minimum(i, ctx_blk - 1), 0)),
                row, _mod_spec(ctx_blk),
                pl.BlockSpec((d, d), lambda i: (0, 0)),
                pl.BlockSpec((2, d), lambda i: (0, 0)),
                pl.BlockSpec((2, d), lambda i: (0, 0)),
                pl.BlockSpec((d, LANES), lambda i: (0, 0))]
    args = [o_lat, xt, modl, w_out, lng, lnb, rw]
    if with_ctx:
        in_specs = [pl.BlockSpec((ROW_BLK, d), lambda i: (0, 0))] + in_specs
        args = [o_ctx] + args
    kern = functools.partial(_post_attn_kernel, with_ctx=with_ctx, ctx_blk=ctx_blk)
    return pl.pallas_call(
        kern,
        grid=(nblk,),
        in_specs=in_specs,
        out_specs=[row, row, pl.BlockSpec((ROW_BLK, LANES), lambda i: (i, 0))],
        out_shape=[jax.ShapeDtypeStruct((nblk * ROW_BLK, d), F32),
                   jax.ShapeDtypeStruct((nblk * ROW_BLK, d), F32),
                   jax.ShapeDtypeStruct((nblk * ROW_BLK, LANES), F32)],
        compiler_params=_cparams(1),
        name="post_attn",
    )(*args)


def _row_copies(n_rows, make):
    def start(r, c):
        for kk in range(2):
            make(r, kk).start(priority=kk)
        return c

    def wait(r, c):
        for kk in range(2):
            make(r, kk).wait()
        return c

    lax.fori_loop(0, n_rows, start, 0)
    lax.fori_loop(0, n_rows, wait, 0)


def _dispatch_kernel(dest_ref, h_ref, xs_in_ref, xs_ref, sem):
    del xs_in_ref

    def make(r, kk):
        dst = dest_ref[0, 0, 2 * r + kk]
        return pltpu.make_async_copy(h_ref.at[pl.ds(r, 1)], xs_ref.at[pl.ds(dst, 1)], sem)

    _row_copies(h_ref.shape[0], make)


def _dispatch(h2, dest3, xs0):
    nblk = dest3.shape[0]
    return pl.pallas_call(
        _dispatch_kernel,
        grid=(nblk,),
        in_specs=[pl.BlockSpec((1, 1, 2 * ROW_BLK), lambda i: (i, 0, 0), memory_space=pltpu.SMEM),
                  pl.BlockSpec((ROW_BLK, D_MODEL), lambda i: (i, 0)),
                  pl.BlockSpec(memory_space=pl.ANY)],
        out_specs=pl.BlockSpec(memory_space=pl.ANY),
        out_shape=jax.ShapeDtypeStruct(xs0.shape, xs0.dtype),
        scratch_shapes=[pltpu.SemaphoreType.DMA(())],
        input_output_aliases={2: 0},
        compiler_params=pltpu.CompilerParams(dimension_semantics=("arbitrary",),
                                             vmem_limit_bytes=VMEM_LIMIT, has_side_effects=True),
        name="moe_dispatch",
    )(dest3, h2, xs0)


def _experts_kernel(bexp_ref, nused_ref, xs_ref, w1_ref, w3_ref, w2_ref, y_ref):
    del bexp_ref
    b = pl.program_id(0)

    @pl.when(b < nused_ref[0])
    def _():
        x = xs_ref[...].astype(BF16)
        h1 = jnp.dot(x, w1_ref[...].astype(BF16), preferred_element_type=F32)
        h3 = jnp.dot(x, w3_ref[...].astype(BF16), preferred_element_type=F32)
        a = (_silu(h1) * h3).astype(BF16)
        y_ref[...] = jnp.dot(a, w2_ref[...].astype(BF16), preferred_element_type=F32)

    @pl.when(b >= nused_ref[0])
    def _():
        y_ref[...] = jnp.zeros(y_ref.shape, F32)


def _experts(block_exp, nused, xs, w1, w3, w2):
    n_rows = xs.shape[0]
    d = D_MODEL
    grid_spec = pltpu.PrefetchScalarGridSpec(
        num_scalar_prefetch=2,
        grid=(n_rows // MOE_BLK,),
        in_specs=[pl.BlockSpec((MOE_BLK, d), lambda b, be, nu: (jnp.minimum(b, nu[0] - 1), 0)),
                  pl.BlockSpec((None, d, D_EXPERT), lambda b, be, nu: (be[b], 0, 0)),
                  pl.BlockSpec((None, d, D_EXPERT), lambda b, be, nu: (be[b], 0, 0)),
                  pl.BlockSpec((None, D_EXPERT, d), lambda b, be, nu: (be[b], 0, 0))],
        out_specs=pl.BlockSpec((MOE_BLK, d), lambda b, be, nu: (b, 0)),
    )
    return pl.pallas_call(
        _experts_kernel,
        grid_spec=grid_spec,
        out_shape=jax.ShapeDtypeStruct((n_rows, d), F32),
        compiler_params=_cparams(1),
        name="moe_experts",
    )(block_exp, nused, xs, w1, w3, w2)


def _combine_kernel(dest_ref, x1_ref, r_ref, mod_ref, lng_ref, lnb_ref, y_ref, o_ref, ybuf, sem):
    def make(r, kk):
        src = dest_ref[0, 0, 2 * r + kk]
        return pltpu.make_async_copy(y_ref.at[pl.ds(src, 1)], ybuf.at[kk, pl.ds(r, 1)], sem)

    _row_copies(x1_ref.shape[0], make)
    ymix = ybuf[0] * r_ref[:, 2:3] + ybuf[1] * r_ref[:, 3:4]
    o_ref[...] = _layer_norm(DN_ALPHA * x1_ref[...] + mod_ref[5:6, :] * ymix, lng_ref[1:2, :], lnb_ref[1:2, :])


def _combine(dest3, x1, r, modl, lng, lnb, y, *, ctx_blk):
    nblk = dest3.shape[0]
    d = D_MODEL
    return pl.pallas_call(
        _combine_kernel,
        grid=(nblk,),
        in_specs=[pl.BlockSpec((1, 1, 2 * ROW_BLK), lambda i: (i, 0, 0), memory_space=pltpu.SMEM),
                  pl.BlockSpec((ROW_BLK, d), lambda i: (i, 0)),
                  pl.BlockSpec((ROW_BLK, LANES), lambda i: (i, 0)),
                  _mod_spec(ctx_blk),
                  pl.BlockSpec((2, d), lambda i: (0, 0)),
                  pl.BlockSpec((2, d), lambda i: (0, 0)),
                  pl.BlockSpec(memory_space=pl.ANY)],
        out_specs=pl.BlockSpec((ROW_BLK, d), lambda i: (i, 0)),
        out_shape=jax.ShapeDtypeStruct((nblk * ROW_BLK, d), F32),
        scratch_shapes=[pltpu.VMEM((2, ROW_BLK, d), F32), pltpu.SemaphoreType.DMA(())],
        compiler_params=_cparams(1),
        name="moe_combine",
    )(dest3, x1, r, modl, lng, lnb, y)


def _dispatch_plan(r):
    ntok = r.shape[0]
    ef = r[:, 0:2].astype(jnp.int32).reshape(-1)
    na = 2 * ntok
    sub = 256
    oh = (ef[:, None] == jnp.arange(N_EXPERTS, dtype=jnp.int32)[None, :]).astype(F32)
    ohb = oh.reshape(na // sub, sub, N_EXPERTS)
    tril = jnp.tril(jnp.ones((sub, sub), F32))
    within = jnp.einsum("ij,bjk->bik", tril, ohb)
    tot = within[:, -1, :]
    before = jnp.cumsum(tot, axis=0) - tot
    rank = jnp.sum((within + before[:, None, :]) * ohb, axis=2).reshape(na) - 1.0
    counts = jnp.sum(tot, axis=0).astype(jnp.int32)
    padded = (counts + MOE_BLK - 1) // MOE_BLK * MOE_BLK
    pad_ends = jnp.cumsum(padded)
    pad_starts = pad_ends - padded
    dest = jnp.sum(oh * pad_starts.astype(F32)[None, :], axis=1) + rank
    dest = dest.astype(jnp.int32)
    n_blocks = na // MOE_BLK + N_EXPERTS
    blk_start = jnp.arange(n_blocks, dtype=jnp.int32) * MOE_BLK
    block_exp = jnp.sum((pad_ends[None, :] <= blk_start[:, None]).astype(jnp.int32), axis=1)
    block_exp = jnp.minimum(block_exp, N_EXPERTS - 1)
    nused = (pad_ends[-1:] // MOE_BLK).astype(jnp.int32)
    return dest.reshape(ntok // ROW_BLK, 1, 2 * ROW_BLK), block_exp, nused, n_blocks


def _moe(h2, x1, r, modl, lng, lnb, w1, w3, w2, *, ctx_blk):
    dest3, block_exp, nused, n_blocks = _dispatch_plan(r)
    xs0 = jnp.zeros((n_blocks * MOE_BLK, D_MODEL), F32)
    xs = _dispatch(h2, dest3, xs0)
    y = _experts(block_exp, nused, xs, w1, w3, w2)
    return _combine(dest3, x1, r, modl, lng, lnb, y, ctx_blk=ctx_blk)


def _rope_tables(n):
    tkn = jnp.arange(n, dtype=jnp.int32)
    row = (tkn // GRID_W).astype(F32)
    col = (tkn % GRID_W).astype(F32)
    half = HEAD_DIM // 2
    inv = ROPE_THETA ** (-jnp.arange(0, half, 2, dtype=F32) / half)
    ang_r = row[:, None] * inv[None, :]
    ang_c = col[:, None] * inv[None, :]
    ang = jnp.concatenate([ang_r, ang_r, ang_c, ang_c] * 2, axis=-1)
    cos = jnp.concatenate([jnp.cos(ang), jnp.ones((CTX_LEN, LANES), F32)], axis=0)
    sin = jnp.concatenate([jnp.sin(ang), jnp.zeros((CTX_LEN, LANES), F32)], axis=0)
    return cos, sin


def _kv_block(t):
    for cand in (1280, 1024, 768, 512, 256):
        if t % cand == 0:
            return cand
    raise ValueError(f"unsupported token count {t}")


def kernel(x, c, ctx, c_ctx, w_mod, b_mod, ln_g, ln_b, w_in_even, w_out_even, diff_lam, diff_subln_g,
           gqa_qk_g, w_in_odd, w_out_odd, na_rpb, router_g, router_e, w1, w3, w2):
    b, n, d = x.shape
    assert b == 1 and d == D_MODEL and ctx.shape == (1, CTX_LEN, D_MODEL)
    assert n % ROW_BLK == 0 and n // GRID_W >= BAND_ROWS and CTX_LEN == ROW_BLK
    t = n + CTX_LEN
    ctx_blk = n // ROW_BLK
    xt = jnp.concatenate([x[0], ctx[0]], axis=0)
    cvec = jnp.zeros((8, d), F32).at[0].set(c[0]).at[1].set(c_ctx)
    mod = _modulation(cvec, w_mod, b_mod)[:, 0:2, :].reshape(DEPTH, 2, 6, d)
    cos, sin = _rope_tables(n)
    hm = jnp.asarray(np.kron(np.eye(2), np.full((HEAD_DIM, HEAD_DIM), 1.0 / HEAD_DIM)), F32)
    rw = jnp.zeros((DEPTH, d, LANES), F32)
    rw = rw.at[:, :, 0:N_GROUPS].set(router_g).at[:, :, N_GROUPS:N_GROUPS + N_EXPERTS].set(router_e)
    bk = _kv_block(t)
    bq = ATTN_BQ if n % ATTN_BQ == 0 else ROW_BLK
    out = None
    for l in range(DEPTH):
        last = l == DEPTH - 1
        i = l // 2
        modl = mod[l]
        if l % 2 == 0:
            lam_init = 0.8 - 0.6 * math.exp(-0.3 * l)
            q, k, v, qn, kmx = _proj_even(xt, modl, w_in_even[i].astype(BF16), cos, sin,
                                          jnp.tile(gqa_qk_g[i], (1, 2)), hm)
            subg = diff_subln_g[i].reshape(LANES, 1)
            o_lat = _attn_even(q, k, v, qn, kmx, diff_lam[i], subg, q_row0=0, n_q=n, k_row0=0, n_k=t,
                               bq=bq, bk=bk, lam_init=lam_init)
            o_ctx = None if last else _attn_even(q, k, v, qn, kmx, diff_lam[i], subg, q_row0=n, n_q=CTX_LEN,
                                                 k_row0=n, n_k=CTX_LEN, bq=CTX_LEN, bk=CTX_LEN,
                                                 lam_init=lam_init)
            w_out = w_out_even[i]
        else:
            q, k, v = _proj_odd(xt, modl, w_in_odd[i].astype(BF16))
            bm = _natten_bias_tables(na_rpb[i], n)
            o_lat = _natten(q, k, v, bm, n)
            o_ctx = None if last else _ctx_attn(q, k, v)
            w_out = w_out_odd[i]
        x1, h2, r = _post_attn(o_ctx, o_lat, xt, modl, w_out.astype(BF16), ln_g[l], ln_b[l], rw[l],
                               with_ctx=not last)
        xt = _moe(h2, x1, r, modl, ln_g[l], ln_b[l], w1[l], w3[l], w2[l], ctx_blk=ctx_blk)
        out = xt
    return out.reshape(1, n, d)
```

```python
import functools
import math

import numpy as np
import jax
import jax.numpy as jnp
from jax import lax
from jax.experimental import pallas as pl
from jax.experimental.pallas import tpu as pltpu

D_MODEL = 1024
DEPTH = 4
GRID_W = 64
CTX_LEN = 256
HEAD_DIM = 64
A_HEADS = 4
B_HEADS = 8
B_KV_HEADS = 2
C_HEADS = D_MODEL // HEAD_DIM
NA_WIN_H = 8
NA_WIN_W = 16
RPB_H = 2 * NA_WIN_H - 1
RPB_W = 2 * NA_WIN_W - 1
N_GROUPS = 4
EXPERTS_PER_GROUP = 8
N_EXPERTS = N_GROUPS * EXPERTS_PER_GROUP
D_EXPERT = 512
ROPE_THETA = 10000.0
DN_ALPHA = (2.0 * DEPTH) ** 0.25
EVEN_IN = 2304
LN_EPS = 1e-6

LANES = 128
ROW_BLK = 256
MOE_BLK = 256
ROW_COPY_UNROLL = 8
N_MAPS = 16
N_KV_CHUNKS = 5
ATTN_BQ = 512
LOG2E = math.log2(math.e)
SAFE_LOG2_GAP = 80.0
NORM_SLACK = 1.001
BAND_ROWS = 10
NA_QBLK = 2 * GRID_W
NA_STEP_BLKS = 2
NEG = -1e30
VMEM_LIMIT = 48 * 1024 * 1024

F32 = jnp.float32
BF16 = jnp.bfloat16
HIGHEST = lax.Precision.HIGHEST


def _cparams(n_axes):
    return pltpu.CompilerParams(dimension_semantics=("arbitrary",) * n_axes,
                                vmem_limit_bytes=VMEM_LIMIT)


def _dot_nt(a, b):
    return lax.dot_general(a, b, (((1,), (1,)), ((), ())), preferred_element_type=F32)


def _layer_norm(v, g, b):
    mu = jnp.mean(v, axis=-1, keepdims=True)
    d = v - mu
    var = jnp.mean(d * d, axis=-1, keepdims=True)
    return d * lax.rsqrt(var + LN_EPS) * g + b


def _silu(v):
    return v / (1.0 + jnp.exp(-v))


def _mod_kernel(c_ref, w_ref, b_ref, o_ref):
    s = _silu(c_ref[...])
    o_ref[...] = jnp.dot(s, w_ref[...], preferred_element_type=F32, precision=HIGHEST) + b_ref[...]


def _modulation(cvec, w_mod, b_mod):
    d = D_MODEL
    return pl.pallas_call(
        _mod_kernel,
        grid=(DEPTH, 6),
        in_specs=[pl.BlockSpec((8, d), lambda l, j: (0, 0)),
                  pl.BlockSpec((None, d, d), lambda l, j: (l, 0, j)),
                  pl.BlockSpec((None, 1, d), lambda l, j: (l, 0, j))],
        out_specs=pl.BlockSpec((None, 8, d), lambda l, j: (l, 0, j)),
        out_shape=jax.ShapeDtypeStruct((DEPTH, 8, 6 * d), F32),
        compiler_params=_cparams(2),
        name="modulation",
    )(cvec, w_mod, b_mod.reshape(DEPTH, 1, 6 * d))


def _mod_spec(ctx_blk):
    return pl.BlockSpec((None, 6, D_MODEL), lambda i: (jnp.where(i == ctx_blk, 1, 0), 0, 0))


def _rope(z, cos, sin, first16):
    rot = jnp.where(first16, -pltpu.roll(z, LANES - 16, 1), pltpu.roll(z, 16, 1))
    return z * cos + rot * sin


def _proj_even_kernel(x_ref, mod_ref, w_ref, cos_ref, sin_ref, g_ref, hm_ref, qt_ref, k_ref, vt_ref, qn_ref, kmx_ref):
    x = x_ref[...]
    h = (x * (1.0 + mod_ref[1:2, :]) + mod_ref[0:1, :]).astype(BF16)
    y = jnp.dot(h, w_ref[...], preferred_element_type=F32)
    cos = cos_ref[...]
    sin = sin_ref[...]
    lane = lax.broadcasted_iota(jnp.int32, cos.shape, 1)
    first16 = (lane % 32) < 16
    lo = lane < HEAD_DIM
    hm = hm_ref[...]
    scale = HEAD_DIM ** -0.5 * LOG2E

    def chunk(c):
        return y[:, c * LANES:(c + 1) * LANES]

    def qk_norm(z, g):
        ms = jnp.dot(z * z, hm, preferred_element_type=F32, precision=HIGHEST)
        return z * lax.rsqrt(ms + LN_EPS) * g

    def put_t(ref, idx, val):
        ref[idx] = val.T.astype(BF16)

    def put_q(idx, val):
        vb = val.T.astype(BF16)
        qt_ref[idx] = vb
        vf = vb.astype(F32)
        qn_ref[idx:idx + 1, :] = jnp.sqrt(jnp.sum(vf * vf, axis=0, keepdims=True))

    kmax = jnp.zeros(kmx_ref.shape, F32)
    klane = lax.broadcasted_iota(jnp.int32, kmx_ref.shape, 1)

    def put_k(idx, val, kmax):
        kb = val.astype(BF16)
        k_ref[idx] = kb
        kf = kb.astype(F32)
        n2 = jnp.max(jnp.sum(kf * kf, axis=1, keepdims=True), axis=0, keepdims=True)
        return jnp.where(klane == idx, n2, kmax)

    for hd in range(A_HEADS):
        q = _rope(chunk(hd), cos, sin, first16) * scale
        put_q(hd, jnp.where(lo, q, 0.0))
        put_q(A_HEADS + hd, jnp.where(lo, 0.0, q))
        kmax = put_k(hd, _rope(chunk(4 + hd), cos, sin, first16), kmax)
        put_t(vt_ref, hd, chunk(8 + hd))
    gq = g_ref[0:1, :]
    gk = g_ref[1:2, :]
    for pr in range(B_HEADS // 2):
        q = _rope(qk_norm(chunk(12 + pr), gq), cos, sin, first16) * scale
        qs = pltpu.roll(q, HEAD_DIM, 1)
        if pr < 2:
            put_q(8 + 2 * pr, jnp.where(lo, q, 0.0))
            put_q(8 + 2 * pr + 1, jnp.where(lo, qs, 0.0))
        else:
            put_q(8 + 2 * pr, jnp.where(lo, 0.0, qs))
            put_q(8 + 2 * pr + 1, jnp.where(lo, 0.0, q))
    kmax = put_k(4, _rope(qk_norm(chunk(16), gk), cos, sin, first16), kmax)
    put_t(vt_ref, 4, chunk(17))
    kmx_ref[...] = kmax


def _proj_even(xt, modl, w_in, cos, sin, g2, hm):
    t = xt.shape[0]
    nb = t // ROW_BLK
    return pl.pallas_call(
        _proj_even_kernel,
        grid=(nb,),
        in_specs=[pl.BlockSpec((ROW_BLK, D_MODEL), lambda i: (i, 0)),
                  _mod_spec(nb - 1),
                  pl.BlockSpec((D_MODEL, EVEN_IN), lambda i: (0, 0)),
                  pl.BlockSpec((ROW_BLK, LANES), lambda i: (i, 0)),
                  pl.BlockSpec((ROW_BLK, LANES), lambda i: (i, 0)),
                  pl.BlockSpec((2, LANES), lambda i: (0, 0)),
                  pl.BlockSpec((LANES, LANES), lambda i: (0, 0))],
        out_specs=[pl.BlockSpec((N_MAPS, LANES, ROW_BLK), lambda i: (0, 0, i)),
                   pl.BlockSpec((N_KV_CHUNKS, ROW_BLK, LANES), lambda i: (0, i, 0)),
                   pl.BlockSpec((N_KV_CHUNKS, LANES, ROW_BLK), lambda i: (0, 0, i)),
                   pl.BlockSpec((N_MAPS, ROW_BLK), lambda i: (0, i)),
                   pl.BlockSpec((None, 8, LANES), lambda i: (i, 0, 0))],
        out_shape=[jax.ShapeDtypeStruct((N_MAPS, LANES, t), BF16),
                   jax.ShapeDtypeStruct((N_KV_CHUNKS, t, LANES), BF16),
                   jax.ShapeDtypeStruct((N_KV_CHUNKS, LANES, t), BF16),
                   jax.ShapeDtypeStruct((N_MAPS, t), F32),
                   jax.ShapeDtypeStruct((nb, 8, LANES), F32)],
        compiler_params=_cparams(1),
        name="proj_even",
    )(xt, modl, w_in, cos, sin, g2, hm)


def _kv_chunk(mm):
    return mm % A_HEADS if mm < 2 * A_HEADS else A_HEADS


def _attn_even_kernel(kmax_ref, lam_ref, subg_ref, qn_ref, qt_ref, k_ref, vt_ref, o_ref, m_sc, l_sc, acc_sc, *,
                      nkv, lam_init):
    j = pl.program_id(1)

    @pl.when(j == 0)
    def _():
        m_sc[...] = jnp.full(m_sc.shape, NEG, F32)
        l_sc[...] = jnp.zeros(l_sc.shape, F32)
        acc_sc[...] = jnp.zeros(acc_sc.shape, F32)

    gap = None
    for mm in range(N_MAPS):
        ub = qn_ref[mm:mm + 1, :] * (kmax_ref[j, _kv_chunk(mm)] * NORM_SLACK)
        g = ub - m_sc[mm]
        gap = g if gap is None else jnp.maximum(gap, g)
    keep_ref = jnp.max(gap) < SAFE_LOG2_GAP

    def scores(mm):
        return jnp.dot(k_ref[_kv_chunk(mm)], qt_ref[mm], preferred_element_type=F32)

    def accumulate_keep(mm, s):
        p = jnp.exp2(s - m_sc[mm])
        l_sc[mm] = l_sc[mm] + jnp.sum(p, axis=0, keepdims=True)
        acc_sc[mm] = acc_sc[mm] + jnp.dot(vt_ref[_kv_chunk(mm)], p.astype(BF16), preferred_element_type=F32)

    def accumulate_move(mm, s):
        m_prev = m_sc[mm]
        m_new = jnp.maximum(m_prev, jnp.max(s, axis=0, keepdims=True))
        alpha = jnp.exp2(m_prev - m_new)
        p = jnp.exp2(s - m_new)
        l_sc[mm] = alpha * l_sc[mm] + jnp.sum(p, axis=0, keepdims=True)
        pv = jnp.dot(vt_ref[_kv_chunk(mm)], p.astype(BF16), preferred_element_type=F32)
        acc_sc[mm] = alpha * acc_sc[mm] + pv
        m_sc[mm] = m_new

    def all_maps(accumulate):
        s_next = scores(0)
        for mm in range(N_MAPS):
            s_cur = s_next
            if mm + 1 < N_MAPS:
                s_next = scores(mm + 1)
            accumulate(mm, s_cur)

    @pl.when(keep_ref)
    def _():
        all_maps(accumulate_keep)

    @pl.when(jnp.logical_not(keep_ref))
    def _():
        all_maps(accumulate_move)

    @pl.when(j == nkv - 1)
    def _():
        lamv = lam_ref[...]
        e1 = jnp.exp(jnp.sum(lamv[0:1, :] * lamv[1:2, :], axis=1, keepdims=True))
        e2 = jnp.exp(jnp.sum(lamv[2:3, :] * lamv[3:4, :], axis=1, keepdims=True))
        lam = e1 - e2 + lam_init
        subg = subg_ref[...]
        for hd in range(A_HEADS):
            o = acc_sc[hd] / l_sc[hd] - lam * (acc_sc[A_HEADS + hd] / l_sc[A_HEADS + hd])
            ms = jnp.mean(o * o, axis=0, keepdims=True)
            o = o * lax.rsqrt(ms + LN_EPS) * subg * (1.0 - lam_init)
            o_ref[:, hd * LANES:(hd + 1) * LANES] = o.T.astype(BF16)
        for pr in range(B_HEADS // 2):
            oa = acc_sc[8 + 2 * pr] / l_sc[8 + 2 * pr]
            ob = acc_sc[8 + 2 * pr + 1] / l_sc[8 + 2 * pr + 1]
            r0 = 0 if pr < 2 else HEAD_DIM
            o = jnp.concatenate([oa[r0:r0 + HEAD_DIM, :], ob[r0:r0 + HEAD_DIM, :]], axis=0)
            o_ref[:, (A_HEADS + pr) * LANES:(A_HEADS + pr + 1) * LANES] = o.T.astype(BF16)


def _attn_even(qt, k, vt, qn, kmx, lamf, subg, *, q_row0, n_q, k_row0, n_k, bq, bk, lam_init):
    assert q_row0 % bq == 0 and n_q % bq == 0 and k_row0 % bk == 0 and n_k % bk == 0 and bk % ROW_BLK == 0
    nq = n_q // bq
    nkv = n_k // bk
    qoff = q_row0 // bq
    koff = k_row0 // bk
    kmax = kmx[k_row0 // ROW_BLK:(k_row0 + n_k) // ROW_BLK, 0, 0:8]
    kmax = jnp.sqrt(jnp.max(kmax.reshape(nkv, bk // ROW_BLK, 8), axis=1))
    kern = functools.partial(_attn_even_kernel, nkv=nkv, lam_init=lam_init)
    return pl.pallas_call(
        kern,
        grid=(nq, nkv),
        in_specs=[pl.BlockSpec(memory_space=pltpu.SMEM),
                  pl.BlockSpec((4, HEAD_DIM), lambda i, j: (0, 0)),
                  pl.BlockSpec((LANES, 1), lambda i, j: (0, 0)),
                  pl.BlockSpec((N_MAPS, bq), lambda i, j: (0, i + qoff)),
                  pl.BlockSpec((N_MAPS, LANES, bq), lambda i, j: (0, 0, i + qoff)),
                  pl.BlockSpec((N_KV_CHUNKS, bk, LANES), lambda i, j: (0, j + koff, 0)),
                  pl.BlockSpec((N_KV_CHUNKS, LANES, bk), lambda i, j: (0, 0, j + koff))],
        out_specs=pl.BlockSpec((bq, D_MODEL), lambda i, j: (i, 0)),
        out_shape=jax.ShapeDtypeStruct((n_q, D_MODEL), BF16),
        scratch_shapes=[pltpu.VMEM((N_MAPS, 1, bq), F32), pltpu.VMEM((N_MAPS, 1, bq), F32),
                        pltpu.VMEM((N_MAPS, LANES, bq), F32)],
        compiler_params=_cparams(2),
        name="attn_even",
    )(kmax, lamf, subg, qn, qt, k, vt)


def _proj_odd_kernel(x_ref, mod_ref, w_ref, q_ref, k_ref, v_ref):
    x = x_ref[...]
    h = (x * (1.0 + mod_ref[1:2, :]) + mod_ref[0:1, :]).astype(BF16)
    d = D_MODEL
    q_ref[...] = (jnp.dot(h, w_ref[:, 0:d], preferred_element_type=F32) * (HEAD_DIM ** -0.5)).astype(BF16)
    k_ref[...] = jnp.dot(h, w_ref[:, d:2 * d], preferred_element_type=F32).astype(BF16)
    v_ref[...] = jnp.dot(h, w_ref[:, 2 * d:3 * d], preferred_element_type=F32).astype(BF16)


def _proj_odd(xt, modl, w_in):
    t = xt.shape[0]
    d = D_MODEL
    spec = pl.BlockSpec((ROW_BLK, d), lambda i: (i, 0))
    return pl.pallas_call(
        _proj_odd_kernel,
        grid=(t // ROW_BLK,),
        in_specs=[spec, _mod_spec(t // ROW_BLK - 1), pl.BlockSpec((d, 3 * d), lambda i: (0, 0))],
        out_specs=[spec, spec, spec],
        out_shape=[jax.ShapeDtypeStruct((t, d), BF16)] * 3,
        compiler_params=_cparams(1),
        name="proj_odd",
    )(xt, modl, w_in)


def _softmax_av(parts):
    m = None
    for s, _ in parts:
        sm = jnp.max(s, axis=1, keepdims=True)
        m = sm if m is None else jnp.maximum(m, sm)
    l = None
    o = None
    for s, v in parts:
        p = jnp.exp(s - m)
        ps = jnp.sum(p, axis=1, keepdims=True)
        pv = jnp.dot(p.astype(BF16), v, preferred_element_type=F32)
        l = ps if l is None else l + ps
        o = pv if o is None else o + pv
    return o / l


def _natten_kernel(q_ref, k_ref, v_ref, bm_ref, o_ref, *, rows, nqb):
    n = rows * GRID_W
    nband = BAND_ROWS * GRID_W
    kc = k_ref[n:n + CTX_LEN, :]
    vc = v_ref[n:n + CTX_LEN, :]
    lane = lax.broadcasted_iota(jnp.int32, (NA_QBLK, LANES), 1)
    lo = lane < HEAD_DIM
    work = []
    for sub in range(NA_STEP_BLKS):
        b = pl.program_id(1) * NA_STEP_BLKS + sub
        band = jnp.clip(2 * b - NA_WIN_H // 2, 0, rows - BAND_ROWS)
        typ = jnp.where(b < 2, b, jnp.where(b >= nqb - 2, b - (nqb - 5), 2))
        start = pl.multiple_of(band * GRID_W, GRID_W)
        kb = k_ref[pl.ds(start, nband), :]
        vb = v_ref[pl.ds(start, nband), :]
        q = q_ref[sub * NA_QBLK:(sub + 1) * NA_QBLK, :]
        zero = jnp.zeros_like(q)
        for half in range(2):
            qh = jnp.where(lo, q, zero) if half == 0 else jnp.where(lo, zero, q)
            work.append((_dot_nt(qh, kb), _dot_nt(qh, kc), typ, half, vb))
    outs = [_softmax_av([(s_nb + bm_ref[typ, half], vb), (s_cx, vc)]) for s_nb, s_cx, typ, half, vb in work]
    for sub in range(NA_STEP_BLKS):
        o_ref[sub * NA_QBLK:(sub + 1) * NA_QBLK, :] = jnp.where(lo, outs[2 * sub], outs[2 * sub + 1]).astype(BF16)


def _natten(q, k, v, bm, n):
    t = q.shape[0]
    rows = n // GRID_W
    nqb = n // NA_QBLK
    assert nqb % NA_STEP_BLKS == 0 and nqb >= 5
    npair = C_HEADS // 2
    step_rows = NA_STEP_BLKS * NA_QBLK
    kern = functools.partial(_natten_kernel, rows=rows, nqb=nqb)
    return pl.pallas_call(
        kern,
        grid=(npair, nqb // NA_STEP_BLKS),
        in_specs=[pl.BlockSpec((step_rows, LANES), lambda j, b: (b, j)),
                  pl.BlockSpec((t, LANES), lambda j, b: (0, j)),
                  pl.BlockSpec((t, LANES), lambda j, b: (0, j)),
                  pl.BlockSpec((5, 2, NA_QBLK, BAND_ROWS * GRID_W), lambda j, b: (0, j, 0, 0))],
        out_specs=pl.BlockSpec((step_rows, LANES), lambda j, b: (b, j)),
        out_shape=jax.ShapeDtypeStruct((n, D_MODEL), BF16),
        compiler_params=_cparams(2),
        name="natten",
    )(q, k, v, bm)


def _ctx_attn_kernel(q_ref, k_ref, v_ref, o_ref):
    q = q_ref[...]
    k = k_ref[...]
    v = v_ref[...]
    lane = lax.broadcasted_iota(jnp.int32, q.shape, 1)
    lo = lane < HEAD_DIM
    zero = jnp.zeros_like(q)
    outs = []
    for half in range(2):
        qh = jnp.where(lo, q, zero) if half == 0 else jnp.where(lo, zero, q)
        outs.append(_softmax_av([(_dot_nt(qh, k), v)]))
    o_ref[...] = jnp.where(lo, outs[0], outs[1]).astype(BF16)


def _ctx_attn(q, k, v):
    ctx_blk = q.shape[0] // CTX_LEN - 1
    spec = pl.BlockSpec((CTX_LEN, LANES), lambda j: (ctx_blk, j))
    return pl.pallas_call(
        _ctx_attn_kernel,
        grid=(C_HEADS // 2,),
        in_specs=[spec, spec, spec],
        out_specs=pl.BlockSpec((CTX_LEN, LANES), lambda j: (0, j)),
        out_shape=jax.ShapeDtypeStruct((CTX_LEN, D_MODEL), BF16),
        compiler_params=_cparams(1),
        name="ctx_attn",
    )(q, k, v)


def _natten_bias_tables(rpb, n):
    rows = n // GRID_W
    nb = rows // 2
    c = np.arange(GRID_W)
    sel = (np.arange(RPB_W)[:, None, None] == (c[None, None, :] - c[None, :, None] + NA_WIN_W - 1))
    toep = jnp.einsum("hdm,mck->hdck", rpb, jnp.asarray(sel, F32), precision=HIGHEST)
    cs = np.clip(c - NA_WIN_W // 2, 0, GRID_W - NA_WIN_W)
    kc = np.arange(GRID_W)
    col_ok = (kc[None, :] >= cs[:, None]) & (kc[None, :] < cs[:, None] + NA_WIN_W)
    colmask = jnp.asarray(np.where(col_ok, 0.0, NEG), F32)
    negblk = jnp.full((C_HEADS, GRID_W, GRID_W), NEG, F32)
    tables = []
    for b in (0, 1, 2, nb - 2, nb - 1):
        band = int(np.clip(2 * b - NA_WIN_H // 2, 0, rows - BAND_ROWS))
        qrows = []
        for qr in range(2):
            r = 2 * b + qr
            rs = int(np.clip(r - NA_WIN_H // 2, 0, rows - NA_WIN_H))
            blks = []
            for i in range(BAND_ROWS):
                kr = band + i
                if rs <= kr < rs + NA_WIN_H:
                    blks.append(toep[:, kr - r + NA_WIN_H - 1] + colmask[None])
                else:
                    blks.append(negblk)
            qrows.append(jnp.concatenate(blks, axis=2))
        tables.append(jnp.concatenate(qrows, axis=1))
    return jnp.stack(tables, axis=0)


def _post_attn_kernel(*refs, with_ctx, ctx_blk):
    if with_ctx:
        octx_ref, refs = refs[0], refs[1:]
    olat_ref, x_ref, mod_ref, wout_ref, lng_ref, lnb_ref, rw_ref, x1_ref, h2_ref, r_ref = refs
    o = olat_ref[...]
    if with_ctx:
        o = jnp.where(pl.program_id(0) == ctx_blk, octx_ref[...], o)
    mix = jnp.dot(o, wout_ref[...], preferred_element_type=F32)
    x1 = _layer_norm(DN_ALPHA * x_ref[...] + mod_ref[2:3, :] * mix, lng_ref[0:1, :], lnb_ref[0:1, :])
    x1_ref[...] = x1
    h2 = x1 * (1.0 + mod_ref[4:5, :]) + mod_ref[3:4, :]
    h2_ref[...] = h2
    logits = jnp.dot(h2, rw_ref[...], preferred_element_type=F32, precision=HIGHEST)
    lane = lax.broadcasted_iota(jnp.int32, logits.shape, 1).astype(F32)
    big = float(LANES)
    lg = jnp.where(lane < N_GROUPS, logits, NEG)
    mg = jnp.max(lg, axis=1, keepdims=True)
    gstar = jnp.min(jnp.where(lg == mg, lane, big), axis=1, keepdims=True)
    p_top = 1.0 / jnp.sum(jnp.exp(lg - mg), axis=1, keepdims=True)
    e_lo = N_GROUPS + EXPERTS_PER_GROUP * gstar
    le = jnp.where((lane >= e_lo) & (lane < e_lo + EXPERTS_PER_GROUP), logits, NEG)
    v1 = jnp.max(le, axis=1, keepdims=True)
    i1 = jnp.min(jnp.where(le == v1, lane, big), axis=1, keepdims=True)
    le2 = jnp.where(lane == i1, NEG, le)
    v2 = jnp.max(le2, axis=1, keepdims=True)
    i2 = jnp.min(jnp.where(le2 == v2, lane, big), axis=1, keepdims=True)
    e2 = jnp.exp(v2 - v1)
    w1 = p_top / (1.0 + e2)
    w2 = p_top * e2 / (1.0 + e2)
    r = jnp.where(lane == 0.0, i1 - N_GROUPS, 0.0)
    r = jnp.where(lane == 1.0, i2 - N_GROUPS, r)
    r = jnp.where(lane == 2.0, w1, r)
    r = jnp.where(lane == 3.0, w2, r)
    r_ref[...] = r


def _post_attn(o_ctx, o_lat, xt, modl, w_out, lng, lnb, rw, *, with_ctx):
    t = xt.shape[0]
    d = D_MODEL
    ctx_blk = (t - CTX_LEN) // ROW_BLK
    nblk = ctx_blk + 1 if with_ctx else ctx_blk
    row = pl.BlockSpec((ROW_BLK, d), lambda i: (i, 0))
    in_specs = [pl.BlockSpec((ROW_BLK, d), lambda i: (jnp.minimum(i, ctx_blk - 1), 0)),
                row, _mod_spec(ctx_blk),
                pl.BlockSpec((d, d), lambda i: (0, 0)),
                pl.BlockSpec((2, d), lambda i: (0, 0)),
                pl.BlockSpec((2, d), lambda i: (0, 0)),
                pl.BlockSpec((d, LANES), lambda i: (0, 0))]
    args = [o_lat, xt, modl, w_out, lng, lnb, rw]
    if with_ctx:
        in_specs = [pl.BlockSpec((ROW_BLK, d), lambda i: (0, 0))] + in_specs
        args = [o_ctx] + args
    kern = functools.partial(_post_attn_kernel, with_ctx=with_ctx, ctx_blk=ctx_blk)
    return pl.pallas_call(
        kern,
        grid=(nblk,),
        in_specs=in_specs,
        out_specs=[row, row, pl.BlockSpec((ROW_BLK, LANES), lambda i: (i, 0))],
        out_shape=[jax.ShapeDtypeStruct((nblk * ROW_BLK, d), F32),
                   jax.ShapeDtypeStruct((nblk * ROW_BLK, d), F32),
                   jax.ShapeDtypeStruct((nblk * ROW_BLK, LANES), F32)],
        compiler_params=_cparams(1),
        name="post_attn",
    )(*args)


def _row_copies(n_rows, make):
    def start(r, c):
        for kk in range(2):
            make(r, kk).start(priority=kk)
        return c

    def wait(r, c):
        for kk in range(2):
            make(r, kk).wait()
        return c

    lax.fori_loop(0, n_rows, start, 0, unroll=ROW_COPY_UNROLL)
    lax.fori_loop(0, n_rows, wait, 0, unroll=ROW_COPY_UNROLL)


def _dispatch_kernel(dest_ref, h_ref, xs_in_ref, xs_ref, sem):
    del xs_in_ref

    def make(r, kk):
        dst = dest_ref[0, 0, 2 * r + kk]
        return pltpu.make_async_copy(h_ref.at[pl.ds(r, 1)], xs_ref.at[pl.ds(dst, 1)], sem)

    _row_copies(h_ref.shape[0], make)


def _dispatch(h2, dest3, xs0):
    nblk = dest3.shape[0]
    return pl.pallas_call(
        _dispatch_kernel,
        grid=(nblk,),
        in_specs=[pl.BlockSpec((1, 1, 2 * ROW_BLK), lambda i: (i, 0, 0), memory_space=pltpu.SMEM),
                  pl.BlockSpec((ROW_BLK, D_MODEL), lambda i: (i, 0)),
                  pl.BlockSpec(memory_space=pl.ANY)],
        out_specs=pl.BlockSpec(memory_space=pl.ANY),
        out_shape=jax.ShapeDtypeStruct(xs0.shape, xs0.dtype),
        scratch_shapes=[pltpu.SemaphoreType.DMA(())],
        input_output_aliases={2: 0},
        compiler_params=pltpu.CompilerParams(dimension_semantics=("arbitrary",),
                                             vmem_limit_bytes=VMEM_LIMIT, has_side_effects=True),
        name="moe_dispatch",
    )(dest3, h2, xs0)


def _experts_kernel(bexp_ref, nused_ref, xs_ref, w1_ref, w3_ref, w2_ref, y_ref):
    del bexp_ref
    b = pl.program_id(0)

    @pl.when(b < nused_ref[0])
    def _():
        x = xs_ref[...].astype(BF16)
        h1 = jnp.dot(x, w1_ref[...].astype(BF16), preferred_element_type=F32)
        h3 = jnp.dot(x, w3_ref[...].astype(BF16), preferred_element_type=F32)
        a = (_silu(h1) * h3).astype(BF16)
        y_ref[...] = jnp.dot(a, w2_ref[...].astype(BF16), preferred_element_type=F32)

    @pl.when(b >= nused_ref[0])
    def _():
        y_ref[...] = jnp.zeros(y_ref.shape, F32)


def _experts(block_exp, nused, xs, w1, w3, w2, layer):
    n_rows = xs.shape[0]
    d = D_MODEL
    grid_spec = pltpu.PrefetchScalarGridSpec(
        num_scalar_prefetch=2,
        grid=(n_rows // MOE_BLK,),
        in_specs=[pl.BlockSpec((MOE_BLK, d), lambda b, be, nu: (jnp.minimum(b, nu[0] - 1), 0)),
                  pl.BlockSpec((None, None, d, D_EXPERT), lambda b, be, nu: (layer, be[b], 0, 0)),
                  pl.BlockSpec((None, None, d, D_EXPERT), lambda b, be, nu: (layer, be[b], 0, 0)),
                  pl.BlockSpec((None, None, D_EXPERT, d), lambda b, be, nu: (layer, be[b], 0, 0))],
        out_specs=pl.BlockSpec((MOE_BLK, d), lambda b, be, nu: (b, 0)),
    )
    return pl.pallas_call(
        _experts_kernel,
        grid_spec=grid_spec,
        out_shape=jax.ShapeDtypeStruct((n_rows, d), F32),
        compiler_params=_cparams(1),
        name="moe_experts",
    )(block_exp, nused, xs, w1, w3, w2)


def _combine_kernel(dest_ref, x1_ref, r_ref, mod_ref, lng_ref, lnb_ref, y_ref, o_ref, ybuf, sem):
    def make(r, kk):
        src = dest_ref[0, 0, 2 * r + kk]
        return pltpu.make_async_copy(y_ref.at[pl.ds(src, 1)], ybuf.at[kk, pl.ds(r, 1)], sem)

    _row_copies(x1_ref.shape[0], make)
    ymix = ybuf[0] * r_ref[:, 2:3] + ybuf[1] * r_ref[:, 3:4]
    o_ref[...] = _layer_norm(DN_ALPHA * x1_ref[...] + mod_ref[5:6, :] * ymix, lng_ref[1:2, :], lnb_ref[1:2, :])


def _combine(dest3, x1, r, modl, lng, lnb, y, *, ctx_blk):
    nblk = dest3.shape[0]
    d = D_MODEL
    return pl.pallas_call(
        _combine_kernel,
        grid=(nblk,),
        in_specs=[pl.BlockSpec((1, 1, 2 * ROW_BLK), lambda i: (i, 0, 0), memory_space=pltpu.SMEM),
                  pl.BlockSpec((ROW_BLK, d), lambda i: (i, 0)),
                  pl.BlockSpec((ROW_BLK, LANES), lambda i: (i, 0)),
                  _mod_spec(ctx_blk),
                  pl.BlockSpec((2, d), lambda i: (0, 0)),
                  pl.BlockSpec((2, d), lambda i: (0, 0)),
                  pl.BlockSpec(memory_space=pl.ANY)],
        out_specs=pl.BlockSpec((ROW_BLK, d), lambda i: (i, 0)),
        out_shape=jax.ShapeDtypeStruct((nblk * ROW_BLK, d), F32),
        scratch_shapes=[pltpu.VMEM((2, ROW_BLK, d), F32), pltpu.SemaphoreType.DMA(())],
        compiler_params=_cparams(1),
        name="moe_combine",
    )(dest3, x1, r, modl, lng, lnb, y)


def _dispatch_plan(r):
    ntok = r.shape[0]
    ef = r[:, 0:2].astype(jnp.int32).reshape(-1)
    na = 2 * ntok
    sub = 256
    oh = (ef[:, None] == jnp.arange(N_EXPERTS, dtype=jnp.int32)[None, :]).astype(F32)
    ohb = oh.reshape(na // sub, sub, N_EXPERTS)
    tril = jnp.tril(jnp.ones((sub, sub), F32))
    within = jnp.einsum("ij,bjk->bik", tril, ohb)
    tot = within[:, -1, :]
    before = jnp.cumsum(tot, axis=0) - tot
    rank = jnp.sum((within + before[:, None, :]) * ohb, axis=2).reshape(na) - 1.0
    counts = jnp.sum(tot, axis=0).astype(jnp.int32)
    padded = (counts + MOE_BLK - 1) // MOE_BLK * MOE_BLK
    pad_ends = jnp.cumsum(padded)
    pad_starts = pad_ends - padded
    dest = jnp.sum(oh * pad_starts.astype(F32)[None, :], axis=1) + rank
    dest = dest.astype(jnp.int32)
    n_blocks = na // MOE_BLK + N_EXPERTS
    blk_start = jnp.arange(n_blocks, dtype=jnp.int32) * MOE_BLK
    block_exp = jnp.sum((pad_ends[None, :] <= blk_start[:, None]).astype(jnp.int32), axis=1)
    block_exp = jnp.minimum(block_exp, N_EXPERTS - 1)
    nused = (pad_ends[-1:] // MOE_BLK).astype(jnp.int32)
    return dest.reshape(ntok // ROW_BLK, 1, 2 * ROW_BLK), block_exp, nused, n_blocks


def _moe(h2, x1, r, modl, lng, lnb, w1, w3, w2, *, layer, ctx_blk):
    dest3, block_exp, nused, n_blocks = _dispatch_plan(r)
    xs0 = jnp.zeros((n_blocks * MOE_BLK, D_MODEL), F32)
    xs = _dispatch(h2, dest3, xs0)
    y = _experts(block_exp, nused, xs, w1, w3, w2, layer)
    return _combine(dest3, x1, r, modl, lng, lnb, y, ctx_blk=ctx_blk)


def _rope_tables(n):
    tkn = jnp.arange(n, dtype=jnp.int32)
    row = (tkn // GRID_W).astype(F32)
    col = (tkn % GRID_W).astype(F32)
    half = HEAD_DIM // 2
    inv = ROPE_THETA ** (-jnp.arange(0, half, 2, dtype=F32) / half)
    ang_r = row[:, None] * inv[None, :]
    ang_c = col[:, None] * inv[None, :]
    ang = jnp.concatenate([ang_r, ang_r, ang_c, ang_c] * 2, axis=-1)
    cos = jnp.concatenate([jnp.cos(ang), jnp.ones((CTX_LEN, LANES), F32)], axis=0)
    sin = jnp.concatenate([jnp.sin(ang), jnp.zeros((CTX_LEN, LANES), F32)], axis=0)
    return cos, sin


def _kv_block(t):
    for cand in (1280, 1024, 768, 512, 256):
        if t % cand == 0:
            return cand
    raise ValueError(f"unsupported token count {t}")


def kernel(x, c, ctx, c_ctx, w_mod, b_mod, ln_g, ln_b, w_in_even, w_out_even, diff_lam, diff_subln_g,
           gqa_qk_g, w_in_odd, w_out_odd, na_rpb, router_g, router_e, w1, w3, w2):
    b, n, d = x.shape
    assert b == 1 and d == D_MODEL and ctx.shape == (1, CTX_LEN, D_MODEL)
    assert n % ROW_BLK == 0 and n // GRID_W >= BAND_ROWS and CTX_LEN == ROW_BLK
    t = n + CTX_LEN
    ctx_blk = n // ROW_BLK
    xt = jnp.concatenate([x[0], ctx[0]], axis=0)
    cvec = jnp.zeros((8, d), F32).at[0].set(c[0]).at[1].set(c_ctx)
    mod = _modulation(cvec, w_mod, b_mod)[:, 0:2, :].reshape(DEPTH, 2, 6, d)
    cos, sin = _rope_tables(n)
    hm = jnp.asarray(np.kron(np.eye(2), np.full((HEAD_DIM, HEAD_DIM), 1.0 / HEAD_DIM)), F32)
    rw = jnp.zeros((DEPTH, d, LANES), F32)
    rw = rw.at[:, :, 0:N_GROUPS].set(router_g).at[:, :, N_GROUPS:N_GROUPS + N_EXPERTS].set(router_e)
    bk = _kv_block(t)
    bq = ATTN_BQ if n % ATTN_BQ == 0 else ROW_BLK
    out = None
    for l in range(DEPTH):
        last = l == DEPTH - 1
        i = l // 2
        modl = mod[l]
        if l % 2 == 0:
            lam_init = 0.8 - 0.6 * math.exp(-0.3 * l)
            q, k, v, qn, kmx = _proj_even(xt, modl, w_in_even[i].astype(BF16), cos, sin,
                                          jnp.tile(gqa_qk_g[i], (1, 2)), hm)
            subg = diff_subln_g[i].reshape(LANES, 1)
            o_lat = _attn_even(q, k, v, qn, kmx, diff_lam[i], subg, q_row0=0, n_q=n, k_row0=0, n_k=t,
                               bq=bq, bk=bk, lam_init=lam_init)
            o_ctx = None if last else _attn_even(q, k, v, qn, kmx, diff_lam[i], subg, q_row0=n, n_q=CTX_LEN,
                                                 k_row0=n, n_k=CTX_LEN, bq=CTX_LEN, bk=CTX_LEN,
                                                 lam_init=lam_init)
            w_out = w_out_even[i]
        else:
            q, k, v = _proj_odd(xt, modl, w_in_odd[i].astype(BF16))
            bm = _natten_bias_tables(na_rpb[i], n)
            o_lat = _natten(q, k, v, bm, n)
            o_ctx = None if last else _ctx_attn(q, k, v)
            w_out = w_out_odd[i]
        x1, h2, r = _post_attn(o_ctx, o_lat, xt, modl, w_out.astype(BF16), ln_g[l], ln_b[l], rw[l],
                               with_ctx=not last)
        xt = _moe(h2, x1, r, modl, ln_g[l], ln_b[l], w1, w3, w2, layer=l, ctx_blk=ctx_blk)
        out = xt
    return out.reshape(1, n, d)
```

```python
import functools
import math

import numpy as np
import jax
import jax.numpy as jnp
from jax import lax
from jax.experimental import pallas as pl
from jax.experimental.pallas import tpu as pltpu

D_MODEL = 1024
DEPTH = 4
GRID_W = 64
CTX_LEN = 256
HEAD_DIM = 64
A_HEADS = 4
B_HEADS = 8
B_KV_HEADS = 2
C_HEADS = D_MODEL // HEAD_DIM
NA_WIN_H = 8
NA_WIN_W = 16
RPB_H = 2 * NA_WIN_H - 1
RPB_W = 2 * NA_WIN_W - 1
N_GROUPS = 4
EXPERTS_PER_GROUP = 8
N_EXPERTS = N_GROUPS * EXPERTS_PER_GROUP
D_EXPERT = 512
ROPE_THETA = 10000.0
DN_ALPHA = (2.0 * DEPTH) ** 0.25
EVEN_IN = 2304
LN_EPS = 1e-6

LANES = 128
ROW_BLK = 256
MOE_BLK = 256
ROW_COPY_UNROLL = 8
N_MAPS = 16
N_KV_CHUNKS = 5
ATTN_BQ = 512
LOG2E = math.log2(math.e)
SAFE_LOG2_GAP = 80.0
NORM_SLACK = 1.001
BAND_ROWS = 10
NA_QBLK = 2 * GRID_W
NA_STEP_BLKS = 2
NEG = -1e30
VMEM_LIMIT = 48 * 1024 * 1024

F32 = jnp.float32
BF16 = jnp.bfloat16
HIGHEST = lax.Precision.HIGHEST


def _cparams(n_axes):
    return pltpu.CompilerParams(dimension_semantics=("arbitrary",) * n_axes,
                                vmem_limit_bytes=VMEM_LIMIT)


def _dot_nt(a, b):
    return lax.dot_general(a, b, (((1,), (1,)), ((), ())), preferred_element_type=F32)


def _layer_norm(v, g, b):
    mu = jnp.mean(v, axis=-1, keepdims=True)
    d = v - mu
    var = jnp.mean(d * d, axis=-1, keepdims=True)
    return d * lax.rsqrt(var + LN_EPS) * g + b


def _silu(v):
    return v / (1.0 + jnp.exp(-v))


def _mod_kernel(c_ref, w_ref, b_ref, o_ref):
    s = _silu(c_ref[...])
    o_ref[...] = jnp.dot(s, w_ref[...], preferred_element_type=F32, precision=HIGHEST) + b_ref[...]


def _modulation(cvec, w_mod, b_mod):
    d = D_MODEL
    return pl.pallas_call(
        _mod_kernel,
        grid=(DEPTH, 6),
        in_specs=[pl.BlockSpec((8, d), lambda l, j: (0, 0)),
                  pl.BlockSpec((None, d, d), lambda l, j: (l, 0, j)),
                  pl.BlockSpec((None, 1, d), lambda l, j: (l, 0, j))],
        out_specs=pl.BlockSpec((None, 8, d), lambda l, j: (l, 0, j)),
        out_shape=jax.ShapeDtypeStruct((DEPTH, 8, 6 * d), F32),
        compiler_params=_cparams(2),
        name="modulation",
    )(cvec, w_mod, b_mod.reshape(DEPTH, 1, 6 * d))


def _mod_spec(ctx_blk):
    return pl.BlockSpec((None, 6, D_MODEL), lambda i: (jnp.where(i == ctx_blk, 1, 0), 0, 0))


def _rope(z, cos, sin, first16):
    rot = jnp.where(first16, -pltpu.roll(z, LANES - 16, 1), pltpu.roll(z, 16, 1))
    return z * cos + rot * sin


def _proj_even_kernel(x_ref, mod_ref, w_ref, cos_ref, sin_ref, g_ref, hm_ref, qt_ref, k_ref, vt_ref, qn_ref, kmx_ref):
    x = x_ref[...]
    h = (x * (1.0 + mod_ref[1:2, :]) + mod_ref[0:1, :]).astype(BF16)
    y = jnp.dot(h, w_ref[...], preferred_element_type=F32)
    cos = cos_ref[...]
    sin = sin_ref[...]
    lane = lax.broadcasted_iota(jnp.int32, cos.shape, 1)
    first16 = (lane % 32) < 16
    lo = lane < HEAD_DIM
    hm = hm_ref[...]
    scale = HEAD_DIM ** -0.5 * LOG2E

    def chunk(c):
        return y[:, c * LANES:(c + 1) * LANES]

    def qk_norm(z, g):
        ms = jnp.dot(z * z, hm, preferred_element_type=F32, precision=HIGHEST)
        return z * lax.rsqrt(ms + LN_EPS) * g

    def put_t(ref, idx, val):
        ref[idx] = val.T.astype(BF16)

    def put_q(idx, val):
        vb = val.T.astype(BF16)
        qt_ref[idx] = vb
        vf = vb.astype(F32)
        qn_ref[idx:idx + 1, :] = jnp.sqrt(jnp.sum(vf * vf, axis=0, keepdims=True))

    kmax = jnp.zeros(kmx_ref.shape, F32)
    klane = lax.broadcasted_iota(jnp.int32, kmx_ref.shape, 1)

    def put_k(idx, val, kmax):
        kb = val.astype(BF16)
        k_ref[idx] = kb
        kf = kb.astype(F32)
        n2 = jnp.max(jnp.sum(kf * kf, axis=1, keepdims=True), axis=0, keepdims=True)
        return jnp.where(klane == idx, n2, kmax)

    for hd in range(A_HEADS):
        q = _rope(chunk(hd), cos, sin, first16) * scale
        put_q(hd, jnp.where(lo, q, 0.0))
        put_q(A_HEADS + hd, jnp.where(lo, 0.0, q))
        kmax = put_k(hd, _rope(chunk(4 + hd), cos, sin, first16), kmax)
        put_t(vt_ref, hd, chunk(8 + hd))
    gq = g_ref[0:1, :]
    gk = g_ref[1:2, :]
    for pr in range(B_HEADS // 2):
        q = _rope(qk_norm(chunk(12 + pr), gq), cos, sin, first16) * scale
        qs = pltpu.roll(q, HEAD_DIM, 1)
        if pr < 2:
            put_q(8 + 2 * pr, jnp.where(lo, q, 0.0))
            put_q(8 + 2 * pr + 1, jnp.where(lo, qs, 0.0))
        else:
            put_q(8 + 2 * pr, jnp.where(lo, 0.0, qs))
            put_q(8 + 2 * pr + 1, jnp.where(lo, 0.0, q))
    kmax = put_k(4, _rope(qk_norm(chunk(16), gk), cos, sin, first16), kmax)
    put_t(vt_ref, 4, chunk(17))
    kmx_ref[...] = kmax


def _proj_even(xt, modl, w_in, cos, sin, g2, hm):
    t = xt.shape[0]
    nb = t // ROW_BLK
    return pl.pallas_call(
        _proj_even_kernel,
        grid=(nb,),
        in_specs=[pl.BlockSpec((ROW_BLK, D_MODEL), lambda i: (i, 0)),
                  _mod_spec(nb - 1),
                  pl.BlockSpec((D_MODEL, EVEN_IN), lambda i: (0, 0)),
                  pl.BlockSpec((ROW_BLK, LANES), lambda i: (i, 0)),
                  pl.BlockSpec((ROW_BLK, LANES), lambda i: (i, 0)),
                  pl.BlockSpec((2, LANES), lambda i: (0, 0)),
                  pl.BlockSpec((LANES, LANES), lambda i: (0, 0))],
        out_specs=[pl.BlockSpec((N_MAPS, LANES, ROW_BLK), lambda i: (0, 0, i)),
                   pl.BlockSpec((N_KV_CHUNKS, ROW_BLK, LANES), lambda i: (0, i, 0)),
                   pl.BlockSpec((N_KV_CHUNKS, LANES, ROW_BLK), lambda i: (0, 0, i)),
                   pl.BlockSpec((N_MAPS, ROW_BLK), lambda i: (0, i)),
                   pl.BlockSpec((None, 8, LANES), lambda i: (i, 0, 0))],
        out_shape=[jax.ShapeDtypeStruct((N_MAPS, LANES, t), BF16),
                   jax.ShapeDtypeStruct((N_KV_CHUNKS, t, LANES), BF16),
                   jax.ShapeDtypeStruct((N_KV_CHUNKS, LANES, t), BF16),
                   jax.ShapeDtypeStruct((N_MAPS, t), F32),
                   jax.ShapeDtypeStruct((nb, 8, LANES), F32)],
        compiler_params=_cparams(1),
        name="proj_even",
    )(xt, modl, w_in, cos, sin, g2, hm)


def _kv_chunk(mm):
    return mm % A_HEADS if mm < 2 * A_HEADS else A_HEADS


def _attn_even_kernel(kmax_ref, lam_ref, subg_ref, qn_ref, qt_ref, k_ref, vt_ref, o_ref, m_sc, l_sc, acc_sc, *,
                      nkv, lam_init):
    j = pl.program_id(1)

    @pl.when(j == 0)
    def _():
        m_sc[...] = jnp.full(m_sc.shape, NEG, F32)
        l_sc[...] = jnp.zeros(l_sc.shape, F32)
        acc_sc[...] = jnp.zeros(acc_sc.shape, F32)

    gap = None
    for mm in range(N_MAPS):
        ub = qn_ref[mm:mm + 1, :] * (kmax_ref[j, _kv_chunk(mm)] * NORM_SLACK)
        g = ub - m_sc[mm]
        gap = g if gap is None else jnp.maximum(gap, g)
    keep_ref = jnp.max(gap) < SAFE_LOG2_GAP

    def scores(mm):
        return jnp.dot(k_ref[_kv_chunk(mm)], qt_ref[mm], preferred_element_type=F32)

    def accumulate_keep(mm, s):
        p = jnp.exp2(s - m_sc[mm])
        l_sc[mm] = l_sc[mm] + jnp.sum(p, axis=0, keepdims=True)
        acc_sc[mm] = acc_sc[mm] + jnp.dot(vt_ref[_kv_chunk(mm)], p.astype(BF16), preferred_element_type=F32)

    def accumulate_move(mm, s):
        m_prev = m_sc[mm]
        m_new = jnp.maximum(m_prev, jnp.max(s, axis=0, keepdims=True))
        alpha = jnp.exp2(m_prev - m_new)
        p = jnp.exp2(s - m_new)
        l_sc[mm] = alpha * l_sc[mm] + jnp.sum(p, axis=0, keepdims=True)
        pv = jnp.dot(vt_ref[_kv_chunk(mm)], p.astype(BF16), preferred_element_type=F32)
        acc_sc[mm] = alpha * acc_sc[mm] + pv
        m_sc[mm] = m_new

    def all_maps(accumulate):
        s_next = scores(0)
        for mm in range(N_MAPS):
            s_cur = s_next
            if mm + 1 < N_MAPS:
                s_next = scores(mm + 1)
            accumulate(mm, s_cur)

    @pl.when(keep_ref)
    def _():
        all_maps(accumulate_keep)

    @pl.when(jnp.logical_not(keep_ref))
    def _():
        all_maps(accumulate_move)

    @pl.when(j == nkv - 1)
    def _():
        lamv = lam_ref[...]
        e1 = jnp.exp(jnp.sum(lamv[0:1, :] * lamv[1:2, :], axis=1, keepdims=True))
        e2 = jnp.exp(jnp.sum(lamv[2:3, :] * lamv[3:4, :], axis=1, keepdims=True))
        lam = e1 - e2 + lam_init
        subg = subg_ref[...]
        for hd in range(A_HEADS):
            o = acc_sc[hd] / l_sc[hd] - lam * (acc_sc[A_HEADS + hd] / l_sc[A_HEADS + hd])
            ms = jnp.mean(o * o, axis=0, keepdims=True)
            o = o * lax.rsqrt(ms + LN_EPS) * subg * (1.0 - lam_init)
            o_ref[:, hd * LANES:(hd + 1) * LANES] = o.T.astype(BF16)
        for pr in range(B_HEADS // 2):
            oa = acc_sc[8 + 2 * pr] / l_sc[8 + 2 * pr]
            ob = acc_sc[8 + 2 * pr + 1] / l_sc[8 + 2 * pr + 1]
            r0 = 0 if pr < 2 else HEAD_DIM
            o = jnp.concatenate([oa[r0:r0 + HEAD_DIM, :], ob[r0:r0 + HEAD_DIM, :]], axis=0)
            o_ref[:, (A_HEADS + pr) * LANES:(A_HEADS + pr + 1) * LANES] = o.T.astype(BF16)


def _attn_even(qt, k, vt, qn, kmx, lamf, subg, *, q_row0, n_q, k_row0, n_k, bq, bk, lam_init):
    assert q_row0 % bq == 0 and n_q % bq == 0 and k_row0 % bk == 0 and n_k % bk == 0 and bk % ROW_BLK == 0
    nq = n_q // bq
    nkv = n_k // bk
    qoff = q_row0 // bq
    koff = k_row0 // bk
    kmax = kmx[k_row0 // ROW_BLK:(k_row0 + n_k) // ROW_BLK, 0, 0:8]
    kmax = jnp.sqrt(jnp.max(kmax.reshape(nkv, bk // ROW_BLK, 8), axis=1))
    kern = functools.partial(_attn_even_kernel, nkv=nkv, lam_init=lam_init)
    return pl.pallas_call(
        kern,
        grid=(nq, nkv),
        in_specs=[pl.BlockSpec(memory_space=pltpu.SMEM),
                  pl.BlockSpec((4, HEAD_DIM), lambda i, j: (0, 0)),
                  pl.BlockSpec((LANES, 1), lambda i, j: (0, 0)),
                  pl.BlockSpec((N_MAPS, bq), lambda i, j: (0, i + qoff)),
                  pl.BlockSpec((N_MAPS, LANES, bq), lambda i, j: (0, 0, i + qoff)),
                  pl.BlockSpec((N_KV_CHUNKS, bk, LANES), lambda i, j: (0, j + koff, 0)),
                  pl.BlockSpec((N_KV_CHUNKS, LANES, bk), lambda i, j: (0, 0, j + koff))],
        out_specs=pl.BlockSpec((bq, D_MODEL), lambda i, j: (i, 0)),
        out_shape=jax.ShapeDtypeStruct((n_q, D_MODEL), BF16),
        scratch_shapes=[pltpu.VMEM((N_MAPS, 1, bq), F32), pltpu.VMEM((N_MAPS, 1, bq), F32),
                        pltpu.VMEM((N_MAPS, LANES, bq), F32)],
        compiler_params=_cparams(2),
        name="attn_even",
    )(kmax, lamf, subg, qn, qt, k, vt)


def _proj_odd_kernel(x_ref, mod_ref, w_ref, q_ref, k_ref, v_ref):
    x = x_ref[...]
    h = (x * (1.0 + mod_ref[1:2, :]) + mod_ref[0:1, :]).astype(BF16)
    d = D_MODEL
    q_ref[...] = (jnp.dot(h, w_ref[:, 0:d], preferred_element_type=F32) * (HEAD_DIM ** -0.5)).astype(BF16)
    k_ref[...] = jnp.dot(h, w_ref[:, d:2 * d], preferred_element_type=F32).astype(BF16)
    v_ref[...] = jnp.dot(h, w_ref[:, 2 * d:3 * d], preferred_element_type=F32).astype(BF16)


def _proj_odd(xt, modl, w_in):
    t = xt.shape[0]
    d = D_MODEL
    spec = pl.BlockSpec((ROW_BLK, d), lambda i: (i, 0))
    return pl.pallas_call(
        _proj_odd_kernel,
        grid=(t // ROW_BLK,),
        in_specs=[spec, _mod_spec(t // ROW_BLK - 1), pl.BlockSpec((d, 3 * d), lambda i: (0, 0))],
        out_specs=[spec, spec, spec],
        out_shape=[jax.ShapeDtypeStruct((t, d), BF16)] * 3,
        compiler_params=_cparams(1),
        name="proj_odd",
    )(xt, modl, w_in)


def _softmax_av(parts):
    m = None
    for s, _ in parts:
        sm = jnp.max(s, axis=1, keepdims=True)
        m = sm if m is None else jnp.maximum(m, sm)
    l = None
    o = None
    for s, v in parts:
        p = jnp.exp(s - m)
        ps = jnp.sum(p, axis=1, keepdims=True)
        pv = jnp.dot(p.astype(BF16), v, preferred_element_type=F32)
        l = ps if l is None else l + ps
        o = pv if o is None else o + pv
    return o / l


def _natten_kernel(q_ref, k_ref, v_ref, bm_ref, o_ref, *, rows, nqb):
    n = rows * GRID_W
    nband = BAND_ROWS * GRID_W
    kc = k_ref[n:n + CTX_LEN, :]
    vc = v_ref[n:n + CTX_LEN, :]
    lane = lax.broadcasted_iota(jnp.int32, (NA_QBLK, LANES), 1)
    lo = lane < HEAD_DIM
    work = []
    for sub in range(NA_STEP_BLKS):
        b = pl.program_id(1) * NA_STEP_BLKS + sub
        band = jnp.clip(2 * b - NA_WIN_H // 2, 0, rows - BAND_ROWS)
        typ = jnp.where(b < 2, b, jnp.where(b >= nqb - 2, b - (nqb - 5), 2))
        start = pl.multiple_of(band * GRID_W, GRID_W)
        kb = k_ref[pl.ds(start, nband), :]
        vb = v_ref[pl.ds(start, nband), :]
        q = q_ref[sub * NA_QBLK:(sub + 1) * NA_QBLK, :]
        zero = jnp.zeros_like(q)
        qs = jnp.concatenate([jnp.where(lo, q, zero), jnp.where(lo, zero, q)], axis=0)
        work.append((_dot_nt(qs, kb), _dot_nt(qs, kc), typ, vb))
    for sub, (s_nb, s_cx, typ, vb) in enumerate(work):
        bias = bm_ref[typ].reshape(2 * NA_QBLK, nband)
        o = _softmax_av([(s_nb + bias, vb), (s_cx, vc)])
        o_ref[sub * NA_QBLK:(sub + 1) * NA_QBLK, :] = jnp.where(lo, o[:NA_QBLK], o[NA_QBLK:]).astype(BF16)


def _natten(q, k, v, bm, n):
    t = q.shape[0]
    rows = n // GRID_W
    nqb = n // NA_QBLK
    assert nqb % NA_STEP_BLKS == 0 and nqb >= 5
    npair = C_HEADS // 2
    step_rows = NA_STEP_BLKS * NA_QBLK
    kern = functools.partial(_natten_kernel, rows=rows, nqb=nqb)
    return pl.pallas_call(
        kern,
        grid=(npair, nqb // NA_STEP_BLKS),
        in_specs=[pl.BlockSpec((step_rows, LANES), lambda j, b: (b, j)),
                  pl.BlockSpec((t, LANES), lambda j, b: (0, j)),
                  pl.BlockSpec((t, LANES), lambda j, b: (0, j)),
                  pl.BlockSpec((5, 2, NA_QBLK, BAND_ROWS * GRID_W), lambda j, b: (0, j, 0, 0))],
        out_specs=pl.BlockSpec((step_rows, LANES), lambda j, b: (b, j)),
        out_shape=jax.ShapeDtypeStruct((n, D_MODEL), BF16),
        compiler_params=_cparams(2),
        name="natten",
    )(q, k, v, bm)


def _ctx_attn_kernel(q_ref, k_ref, v_ref, o_ref):
    q = q_ref[...]
    k = k_ref[...]
    v = v_ref[...]
    lane = lax.broadcasted_iota(jnp.int32, q.shape, 1)
    lo = lane < HEAD_DIM
    zero = jnp.zeros_like(q)
    outs = []
    for half in range(2):
        qh = jnp.where(lo, q, zero) if half == 0 else jnp.where(lo, zero, q)
        outs.append(_softmax_av([(_dot_nt(qh, k), v)]))
    o_ref[...] = jnp.where(lo, outs[0], outs[1]).astype(BF16)


def _ctx_attn(q, k, v):
    ctx_blk = q.shape[0] // CTX_LEN - 1
    spec = pl.BlockSpec((CTX_LEN, LANES), lambda j: (ctx_blk, j))
    return pl.pallas_call(
        _ctx_attn_kernel,
        grid=(C_HEADS // 2,),
        in_specs=[spec, spec, spec],
        out_specs=pl.BlockSpec((CTX_LEN, LANES), lambda j: (0, j)),
        out_shape=jax.ShapeDtypeStruct((CTX_LEN, D_MODEL), BF16),
        compiler_params=_cparams(1),
        name="ctx_attn",
    )(q, k, v)


def _natten_bias_tables(rpb, n):
    rows = n // GRID_W
    nb = rows // 2
    c = np.arange(GRID_W)
    sel = (np.arange(RPB_W)[:, None, None] == (c[None, None, :] - c[None, :, None] + NA_WIN_W - 1))
    toep = jnp.einsum("hdm,mck->hdck", rpb, jnp.asarray(sel, F32), precision=HIGHEST)
    cs = np.clip(c - NA_WIN_W // 2, 0, GRID_W - NA_WIN_W)
    kc = np.arange(GRID_W)
    col_ok = (kc[None, :] >= cs[:, None]) & (kc[None, :] < cs[:, None] + NA_WIN_W)
    colmask = jnp.asarray(np.where(col_ok, 0.0, NEG), F32)
    negblk = jnp.full((C_HEADS, GRID_W, GRID_W), NEG, F32)
    tables = []
    for b in (0, 1, 2, nb - 2, nb - 1):
        band = int(np.clip(2 * b - NA_WIN_H // 2, 0, rows - BAND_ROWS))
        qrows = []
        for qr in range(2):
            r = 2 * b + qr
            rs = int(np.clip(r - NA_WIN_H // 2, 0, rows - NA_WIN_H))
            blks = []
            for i in range(BAND_ROWS):
                kr = band + i
                if rs <= kr < rs + NA_WIN_H:
                    blks.append(toep[:, kr - r + NA_WIN_H - 1] + colmask[None])
                else:
                    blks.append(negblk)
            qrows.append(jnp.concatenate(blks, axis=2))
        tables.append(jnp.concatenate(qrows, axis=1))
    return jnp.stack(tables, axis=0)


def _post_attn_kernel(*refs, with_ctx, ctx_blk):
    if with_ctx:
        octx_ref, refs = refs[0], refs[1:]
    olat_ref, x_ref, mod_ref, wout_ref, lng_ref, lnb_ref, rwh_ref, rwl_ref, x1_ref, h2_ref, r_ref = refs
    o = olat_ref[...]
    if with_ctx:
        o = jnp.where(pl.program_id(0) == ctx_blk, octx_ref[...], o)
    mix = jnp.dot(o, wout_ref[...], preferred_element_type=F32)
    x1 = _layer_norm(DN_ALPHA * x_ref[...] + mod_ref[2:3, :] * mix, lng_ref[0:1, :], lnb_ref[0:1, :])
    x1_ref[...] = x1
    h2 = x1 * (1.0 + mod_ref[4:5, :]) + mod_ref[3:4, :]
    h2_ref[...] = h2
    h_hi = h2.astype(BF16)
    h_lo = (h2 - h_hi.astype(F32)).astype(BF16)
    logits = (jnp.dot(h_hi, rwh_ref[...], preferred_element_type=F32)
              + jnp.dot(h_hi, rwl_ref[...], preferred_element_type=F32)
              + jnp.dot(h_lo, rwh_ref[...], preferred_element_type=F32))
    lane = lax.broadcasted_iota(jnp.int32, logits.shape, 1).astype(F32)
    big = float(LANES)
    lg = jnp.where(lane < N_GROUPS, logits, NEG)
    mg = jnp.max(lg, axis=1, keepdims=True)
    gstar = jnp.min(jnp.where(lg == mg, lane, big), axis=1, keepdims=True)
    p_top = 1.0 / jnp.sum(jnp.exp(lg - mg), axis=1, keepdims=True)
    e_lo = N_GROUPS + EXPERTS_PER_GROUP * gstar
    le = jnp.where((lane >= e_lo) & (lane < e_lo + EXPERTS_PER_GROUP), logits, NEG)
    v1 = jnp.max(le, axis=1, keepdims=True)
    i1 = jnp.min(jnp.where(le == v1, lane, big), axis=1, keepdims=True)
    le2 = jnp.where(lane == i1, NEG, le)
    v2 = jnp.max(le2, axis=1, keepdims=True)
    i2 = jnp.min(jnp.where(le2 == v2, lane, big), axis=1, keepdims=True)
    e2 = jnp.exp(v2 - v1)
    w1 = p_top / (1.0 + e2)
    w2 = p_top * e2 / (1.0 + e2)
    r = jnp.where(lane == 0.0, i1 - N_GROUPS, 0.0)
    r = jnp.where(lane == 1.0, i2 - N_GROUPS, r)
    r = jnp.where(lane == 2.0, w1, r)
    r = jnp.where(lane == 3.0, w2, r)
    r_ref[...] = r


def _post_attn(o_ctx, o_lat, xt, modl, w_out, lng, lnb, rw, *, with_ctx):
    t = xt.shape[0]
    d = D_MODEL
    ctx_blk = (t - CTX_LEN) // ROW_BLK
    nblk = ctx_blk + 1 if with_ctx else ctx_blk
    row = pl.BlockSpec((ROW_BLK, d), lambda i: (i, 0))
    in_specs = [pl.BlockSpec((ROW_BLK, d), lambda i: (jnp.minimum(i, ctx_blk - 1), 0)),
                row, _mod_spec(ctx_blk),
                pl.BlockSpec((d, d), lambda i: (0, 0)),
                pl.BlockSpec((2, d), lambda i: (0, 0)),
                pl.BlockSpec((2, d), lambda i: (0, 0)),
                pl.BlockSpec((d, LANES), lambda i: (0, 0)),
                pl.BlockSpec((d, LANES), lambda i: (0, 0))]
    rw_hi = rw.astype(BF16)
    rw_lo = (rw - rw_hi.astype(F32)).astype(BF16)
    args = [o_lat, xt, modl, w_out, lng, lnb, rw_hi, rw_lo]
    if with_ctx:
        in_specs = [pl.BlockSpec((ROW_BLK, d), lambda i: (0, 0))] + in_specs
        args = [o_ctx] + args
    kern = functools.partial(_post_attn_kernel, with_ctx=with_ctx, ctx_blk=ctx_blk)
    return pl.pallas_call(
        kern,
        grid=(nblk,),
        in_specs=in_specs,
        out_specs=[row, row, pl.BlockSpec((ROW_BLK, LANES), lambda i: (i, 0))],
        out_shape=[jax.ShapeDtypeStruct((nblk * ROW_BLK, d), F32),
                   jax.ShapeDtypeStruct((nblk * ROW_BLK, d), F32),
                   jax.ShapeDtypeStruct((nblk * ROW_BLK, LANES), F32)],
        compiler_params=_cparams(1),
        name="post_attn",
    )(*args)


def _row_copies(n_rows, make):
    def start(r, c):
        for kk in range(2):
            make(r, kk).start(priority=kk)
        return c

    def wait(r, c):
        for kk in range(2):
            make(r, kk).wait()
        return c

    lax.fori_loop(0, n_rows, start, 0, unroll=ROW_COPY_UNROLL)
    lax.fori_loop(0, n_rows, wait, 0, unroll=ROW_COPY_UNROLL)


def _dispatch_kernel(dest_ref, h_ref, xs_in_ref, xs_ref, sem):
    del xs_in_ref

    def make(r, kk):
        dst = dest_ref[0, 0, 2 * r + kk]
        return pltpu.make_async_copy(h_ref.at[pl.ds(r, 1)], xs_ref.at[pl.ds(dst, 1)], sem)

    _row_copies(h_ref.shape[0], make)


def _dispatch(h2, dest3, xs0):
    nblk = dest3.shape[0]
    return pl.pallas_call(
        _dispatch_kernel,
        grid=(nblk,),
        in_specs=[pl.BlockSpec((1, 1, 2 * ROW_BLK), lambda i: (i, 0, 0), memory_space=pltpu.SMEM),
                  pl.BlockSpec((ROW_BLK, D_MODEL), lambda i: (i, 0)),
                  pl.BlockSpec(memory_space=pl.ANY)],
        out_specs=pl.BlockSpec(memory_space=pl.ANY),
        out_shape=jax.ShapeDtypeStruct(xs0.shape, xs0.dtype),
        scratch_shapes=[pltpu.SemaphoreType.DMA(())],
        input_output_aliases={2: 0},
        compiler_params=pltpu.CompilerParams(dimension_semantics=("arbitrary",),
                                             vmem_limit_bytes=VMEM_LIMIT, has_side_effects=True),
        name="moe_dispatch",
    )(dest3, h2, xs0)


def _experts_kernel(bexp_ref, nused_ref, xs_ref, w1_ref, w3_ref, w2_ref, y_ref):
    del bexp_ref
    b = pl.program_id(0)

    @pl.when(b < nused_ref[0])
    def _():
        x = xs_ref[...].astype(BF16)
        h1 = jnp.dot(x, w1_ref[...].astype(BF16), preferred_element_type=F32)
        h3 = jnp.dot(x, w3_ref[...].astype(BF16), preferred_element_type=F32)
        a = (_silu(h1) * h3).astype(BF16)
        y_ref[...] = jnp.dot(a, w2_ref[...].astype(BF16), preferred_element_type=F32)

    @pl.when(b >= nused_ref[0])
    def _():
        y_ref[...] = jnp.zeros(y_ref.shape, F32)


def _experts(block_exp, nused, xs, w1, w3, w2, layer):
    n_rows = xs.shape[0]
    d = D_MODEL
    grid_spec = pltpu.PrefetchScalarGridSpec(
        num_scalar_prefetch=2,
        grid=(n_rows // MOE_BLK,),
        in_specs=[pl.BlockSpec((MOE_BLK, d), lambda b, be, nu: (jnp.minimum(b, nu[0] - 1), 0)),
                  pl.BlockSpec((None, None, d, D_EXPERT), lambda b, be, nu: (layer, be[b], 0, 0)),
                  pl.BlockSpec((None, None, d, D_EXPERT), lambda b, be, nu: (layer, be[b], 0, 0)),
                  pl.BlockSpec((None, None, D_EXPERT, d), lambda b, be, nu: (layer, be[b], 0, 0))],
        out_specs=pl.BlockSpec((MOE_BLK, d), lambda b, be, nu: (b, 0)),
    )
    return pl.pallas_call(
        _experts_kernel,
        grid_spec=grid_spec,
        out_shape=jax.ShapeDtypeStruct((n_rows, d), F32),
        compiler_params=_cparams(1),
        name="moe_experts",
    )(block_exp, nused, xs, w1, w3, w2)


def _combine_kernel(dcur_ref, dnext_ref, x1_ref, r_ref, mod_ref, lng_ref, lnb_ref, y_ref, o_ref, ybuf, sem):
    i = pl.program_id(0)
    n_rows = x1_ref.shape[0]

    def gather(d_ref, dst_slot, r, kk):
        src = d_ref[0, 0, 2 * r + kk]
        return pltpu.make_async_copy(y_ref.at[pl.ds(src, 1)], ybuf.at[dst_slot, kk, pl.ds(r, 1)], sem.at[dst_slot])

    def start_rows(d_ref, dst_slot, r0, count):
        for rr in range(count):
            for kk in range(2):
                gather(d_ref, dst_slot, r0 + rr, kk).start(priority=kk)

    def wait_all(dst_slot):
        def body(r, c):
            for kk in range(2):
                gather(dcur_ref, dst_slot, r, kk).wait()
            return c
        lax.fori_loop(0, n_rows, body, 0, unroll=ROW_COPY_UNROLL)

    @pl.when(i == 0)
    def _():
        def body(g, c):
            start_rows(dcur_ref, 0, g * ROW_COPY_UNROLL, ROW_COPY_UNROLL)
            return c
        lax.fori_loop(0, n_rows // ROW_COPY_UNROLL, body, 0)

    g2 = mod_ref[5:6, :]
    lng = lng_ref[1:2, :]
    lnb = lnb_ref[1:2, :]

    def step(slot):
        other = 1 - slot

        def issue(g, c):
            start_rows(dnext_ref, other, g * ROW_COPY_UNROLL, ROW_COPY_UNROLL)
            return c

        lax.fori_loop(0, n_rows // ROW_COPY_UNROLL, issue, 0)
        wait_all(slot)
        ymix = ybuf[slot, 0] * r_ref[:, 2:3] + ybuf[slot, 1] * r_ref[:, 3:4]
        o_ref[...] = _layer_norm(DN_ALPHA * x1_ref[...] + g2 * ymix, lng, lnb)

        @pl.when(i == pl.num_programs(0) - 1)
        def _():
            wait_all(other)

    @pl.when(lax.rem(i, 2) == 0)
    def _():
        step(0)

    @pl.when(lax.rem(i, 2) == 1)
    def _():
        step(1)


def _combine(dest3, x1, r, modl, lng, lnb, y, *, ctx_blk):
    nblk = dest3.shape[0]
    d = D_MODEL
    return pl.pallas_call(
        _combine_kernel,
        grid=(nblk,),
        in_specs=[pl.BlockSpec((1, 1, 2 * ROW_BLK), lambda i: (i, 0, 0), memory_space=pltpu.SMEM),
                  pl.BlockSpec((1, 1, 2 * ROW_BLK), lambda i: (jnp.minimum(i + 1, nblk - 1), 0, 0),
                               memory_space=pltpu.SMEM),
                  pl.BlockSpec((ROW_BLK, d), lambda i: (i, 0)),
                  pl.BlockSpec((ROW_BLK, LANES), lambda i: (i, 0)),
                  _mod_spec(ctx_blk),
                  pl.BlockSpec((2, d), lambda i: (0, 0)),
                  pl.BlockSpec((2, d), lambda i: (0, 0)),
                  pl.BlockSpec(memory_space=pl.ANY)],
        out_specs=pl.BlockSpec((ROW_BLK, d), lambda i: (i, 0)),
        out_shape=jax.ShapeDtypeStruct((nblk * ROW_BLK, d), F32),
        scratch_shapes=[pltpu.VMEM((2, 2, ROW_BLK, d), F32), pltpu.SemaphoreType.DMA((2,))],
        compiler_params=_cparams(1),
        name="moe_combine",
    )(dest3, dest3, x1, r, modl, lng, lnb, y)


def _dispatch_plan(r):
    ntok = r.shape[0]
    ef = r[:, 0:2].astype(jnp.int32).reshape(-1)
    na = 2 * ntok
    sub = 256
    oh = (ef[:, None] == jnp.arange(N_EXPERTS, dtype=jnp.int32)[None, :]).astype(F32)
    ohb = oh.reshape(na // sub, sub, N_EXPERTS)
    tril = jnp.tril(jnp.ones((sub, sub), F32))
    within = jnp.einsum("ij,bjk->bik", tril, ohb)
    tot = within[:, -1, :]
    before = jnp.cumsum(tot, axis=0) - tot
    rank = jnp.sum((within + before[:, None, :]) * ohb, axis=2).reshape(na) - 1.0
    counts = jnp.sum(tot, axis=0).astype(jnp.int32)
    padded = (counts + MOE_BLK - 1) // MOE_BLK * MOE_BLK
    pad_ends = jnp.cumsum(padded)
    pad_starts = pad_ends - padded
    dest = jnp.sum(oh * pad_starts.astype(F32)[None, :], axis=1) + rank
    dest = dest.astype(jnp.int32)
    n_blocks = na // MOE_BLK + N_EXPERTS
    blk_start = jnp.arange(n_blocks, dtype=jnp.int32) * MOE_BLK
    block_exp = jnp.sum((pad_ends[None, :] <= blk_start[:, None]).astype(jnp.int32), axis=1)
    block_exp = jnp.minimum(block_exp, N_EXPERTS - 1)
    nused = (pad_ends[-1:] // MOE_BLK).astype(jnp.int32)
    return dest.reshape(ntok // ROW_BLK, 1, 2 * ROW_BLK), block_exp, nused, n_blocks


def _moe(h2, x1, r, modl, lng, lnb, w1, w3, w2, *, layer, ctx_blk):
    dest3, block_exp, nused, n_blocks = _dispatch_plan(r)
    xs0 = jnp.zeros((n_blocks * MOE_BLK, D_MODEL), F32)
    xs = _dispatch(h2, dest3, xs0)
    y = _experts(block_exp, nused, xs, w1, w3, w2, layer)
    return _combine(dest3, x1, r, modl, lng, lnb, y, ctx_blk=ctx_blk)


def _rope_tables(n):
    tkn = jnp.arange(n, dtype=jnp.int32)
    row = (tkn // GRID_W).astype(F32)
    col = (tkn % GRID_W).astype(F32)
    half = HEAD_DIM // 2
    inv = ROPE_THETA ** (-jnp.arange(0, half, 2, dtype=F32) / half)
    ang_r = row[:, None] * inv[None, :]
    ang_c = col[:, None] * inv[None, :]
    ang = jnp.concatenate([ang_r, ang_r, ang_c, ang_c] * 2, axis=-1)
    cos = jnp.concatenate([jnp.cos(ang), jnp.ones((CTX_LEN, LANES), F32)], axis=0)
    sin = jnp.concatenate([jnp.sin(ang), jnp.zeros((CTX_LEN, LANES), F32)], axis=0)
    return cos, sin


def _kv_block(t):
    for cand in (1280, 1024, 768, 512, 256):
        if t % cand == 0:
            return cand
    raise ValueError(f"unsupported token count {t}")


def kernel(x, c, ctx, c_ctx, w_mod, b_mod, ln_g, ln_b, w_in_even, w_out_even, diff_lam, diff_subln_g,
           gqa_qk_g, w_in_odd, w_out_odd, na_rpb, router_g, router_e, w1, w3, w2):
    b, n, d = x.shape
    assert b == 1 and d == D_MODEL and ctx.shape == (1, CTX_LEN, D_MODEL)
    assert n % ROW_BLK == 0 and n // GRID_W >= BAND_ROWS and CTX_LEN == ROW_BLK
    t = n + CTX_LEN
    ctx_blk = n // ROW_BLK
    xt = jnp.concatenate([x[0], ctx[0]], axis=0)
    cvec = jnp.zeros((8, d), F32).at[0].set(c[0]).at[1].set(c_ctx)
    mod = _modulation(cvec, w_mod, b_mod)[:, 0:2, :].reshape(DEPTH, 2, 6, d)
    cos, sin = _rope_tables(n)
    hm = jnp.asarray(np.kron(np.eye(2), np.full((HEAD_DIM, HEAD_DIM), 1.0 / HEAD_DIM)), F32)
    rw = jnp.zeros((DEPTH, d, LANES), F32)
    rw = rw.at[:, :, 0:N_GROUPS].set(router_g).at[:, :, N_GROUPS:N_GROUPS + N_EXPERTS].set(router_e)
    bk = _kv_block(t)
    bq = ATTN_BQ if n % ATTN_BQ == 0 else ROW_BLK
    out = None
    for l in range(DEPTH):
        last = l == DEPTH - 1
        i = l // 2
        modl = mod[l]
        if l % 2 == 0:
            lam_init = 0.8 - 0.6 * math.exp(-0.3 * l)
            q, k, v, qn, kmx = _proj_even(xt, modl, w_in_even[i].astype(BF16), cos, sin,
                                          jnp.tile(gqa_qk_g[i], (1, 2)), hm)
            subg = diff_subln_g[i].reshape(LANES, 1)
            o_lat = _attn_even(q, k, v, qn, kmx, diff_lam[i], subg, q_row0=0, n_q=n, k_row0=0, n_k=t,
                               bq=bq, bk=bk, lam_init=lam_init)
            o_ctx = None if last else _attn_even(q, k, v, qn, kmx, diff_lam[i], subg, q_row0=n, n_q=CTX_LEN,
                                                 k_row0=n, n_k=CTX_LEN, bq=CTX_LEN, bk=CTX_LEN,
                                                 lam_init=lam_init)
            w_out = w_out_even[i]
        else:
            q, k, v = _proj_odd(xt, modl, w_in_odd[i].astype(BF16))
            bm = _natten_bias_tables(na_rpb[i], n)
            o_lat = _natten(q, k, v, bm, n)
            o_ctx = None if last else _ctx_attn(q, k, v)
            w_out = w_out_odd[i]
        x1, h2, r = _post_attn(o_ctx, o_lat, xt, modl, w_out.astype(BF16), ln_g[l], ln_b[l], rw[l],
                               with_ctx=not last)
        xt = _moe(h2, x1, r, modl, ln_g[l], ln_b[l], w1, w3, w2, layer=l, ctx_blk=ctx_blk)
        out = xt
    return out.reshape(1, n, d)
```

```python
import functools
import math

import numpy as np
import jax
import jax.numpy as jnp
from jax import lax
from jax.experimental import pallas as pl
from jax.experimental.pallas import tpu as pltpu

D_MODEL = 1024
DEPTH = 4
GRID_W = 64
CTX_LEN = 256
HEAD_DIM = 64
A_HEADS = 4
B_HEADS = 8
B_KV_HEADS = 2
C_HEADS = D_MODEL // HEAD_DIM
NA_WIN_H = 8
NA_WIN_W = 16
RPB_H = 2 * NA_WIN_H - 1
RPB_W = 2 * NA_WIN_W - 1
N_GROUPS = 4
EXPERTS_PER_GROUP = 8
N_EXPERTS = N_GROUPS * EXPERTS_PER_GROUP
D_EXPERT = 512
ROPE_THETA = 10000.0
DN_ALPHA = (2.0 * DEPTH) ** 0.25
EVEN_IN = 2304
LN_EPS = 1e-6

LANES = 128
ROW_BLK = 256
MOE_BLK = 256
ROW_COPY_UNROLL = 8
N_MAPS = 16
N_KV_CHUNKS = 5
ATTN_BQ = 512
LOG2E = math.log2(math.e)
SAFE_LOG2_GAP = 80.0
NORM_SLACK = 1.001
BAND_ROWS = 10
NA_QBLK = 2 * GRID_W
NA_STEP_BLKS = 4
NEG = -1e30
VMEM_LIMIT = 48 * 1024 * 1024

F32 = jnp.float32
BF16 = jnp.bfloat16
HIGHEST = lax.Precision.HIGHEST


def _cparams(n_axes):
    return pltpu.CompilerParams(dimension_semantics=("arbitrary",) * n_axes,
                                vmem_limit_bytes=VMEM_LIMIT)


def _dot_nt(a, b):
    return lax.dot_general(a, b, (((1,), (1,)), ((), ())), preferred_element_type=F32)


def _layer_norm(v, g, b):
    mu = jnp.mean(v, axis=-1, keepdims=True)
    d = v - mu
    var = jnp.mean(d * d, axis=-1, keepdims=True)
    return d * lax.rsqrt(var + LN_EPS) * g + b


def _silu(v):
    return v / (1.0 + jnp.exp(-v))


def _mod_kernel(c_ref, w_ref, b_ref, o_ref):
    s = _silu(c_ref[...])
    o_ref[...] = jnp.dot(s, w_ref[...], preferred_element_type=F32, precision=HIGHEST) + b_ref[...]


def _modulation(cvec, w_mod, b_mod):
    d = D_MODEL
    return pl.pallas_call(
        _mod_kernel,
        grid=(DEPTH, 6),
        in_specs=[pl.BlockSpec((8, d), lambda l, j: (0, 0)),
                  pl.BlockSpec((None, d, d), lambda l, j: (l, 0, j)),
                  pl.BlockSpec((None, 1, d), lambda l, j: (l, 0, j))],
        out_specs=pl.BlockSpec((None, 8, d), lambda l, j: (l, 0, j)),
        out_shape=jax.ShapeDtypeStruct((DEPTH, 8, 6 * d), F32),
        compiler_params=_cparams(2),
        name="modulation",
    )(cvec, w_mod, b_mod.reshape(DEPTH, 1, 6 * d))


def _mod_spec(ctx_blk):
    return pl.BlockSpec((None, 6, D_MODEL), lambda i: (jnp.where(i == ctx_blk, 1, 0), 0, 0))


def _rope(z, cos, sin, first16):
    rot = jnp.where(first16, -pltpu.roll(z, LANES - 16, 1), pltpu.roll(z, 16, 1))
    return z * cos + rot * sin


def _proj_even_kernel(x_ref, mod_ref, w_ref, cos_ref, sin_ref, g_ref, hm_ref, qt_ref, k_ref, vt_ref, qn_ref, kmx_ref):
    x = x_ref[...]
    h = (x * (1.0 + mod_ref[1:2, :]) + mod_ref[0:1, :]).astype(BF16)
    y = jnp.dot(h, w_ref[...], preferred_element_type=F32)
    cos = cos_ref[...]
    sin = sin_ref[...]
    lane = lax.broadcasted_iota(jnp.int32, cos.shape, 1)
    first16 = (lane % 32) < 16
    lo = lane < HEAD_DIM
    hm = hm_ref[...]
    scale = HEAD_DIM ** -0.5 * LOG2E

    def chunk(c):
        return y[:, c * LANES:(c + 1) * LANES]

    def qk_norm(z, g):
        ms = jnp.dot(z * z, hm, preferred_element_type=F32, precision=HIGHEST)
        return z * lax.rsqrt(ms + LN_EPS) * g

    def put_t(ref, idx, val):
        ref[idx] = val.T.astype(BF16)

    def put_q(idx, val):
        vb = val.T.astype(BF16)
        qt_ref[idx] = vb
        vf = vb.astype(F32)
        qn_ref[idx:idx + 1, :] = jnp.sqrt(jnp.sum(vf * vf, axis=0, keepdims=True))

    kmax = jnp.zeros(kmx_ref.shape, F32)
    klane = lax.broadcasted_iota(jnp.int32, kmx_ref.shape, 1)

    def put_k(idx, val, kmax):
        kb = val.astype(BF16)
        k_ref[idx] = kb
        kf = kb.astype(F32)
        n2 = jnp.max(jnp.sum(kf * kf, axis=1, keepdims=True), axis=0, keepdims=True)
        return jnp.where(klane == idx, n2, kmax)

    for hd in range(A_HEADS):
        q = _rope(chunk(hd), cos, sin, first16) * scale
        put_q(hd, jnp.where(lo, q, 0.0))
        put_q(A_HEADS + hd, jnp.where(lo, 0.0, q))
        kmax = put_k(hd, _rope(chunk(4 + hd), cos, sin, first16), kmax)
        put_t(vt_ref, hd, chunk(8 + hd))
    gq = g_ref[0:1, :]
    gk = g_ref[1:2, :]
    for pr in range(B_HEADS // 2):
        q = _rope(qk_norm(chunk(12 + pr), gq), cos, sin, first16) * scale
        qs = pltpu.roll(q, HEAD_DIM, 1)
        if pr < 2:
            put_q(8 + 2 * pr, jnp.where(lo, q, 0.0))
            put_q(8 + 2 * pr + 1, jnp.where(lo, qs, 0.0))
        else:
            put_q(8 + 2 * pr, jnp.where(lo, 0.0, qs))
            put_q(8 + 2 * pr + 1, jnp.where(lo, 0.0, q))
    kmax = put_k(4, _rope(qk_norm(chunk(16), gk), cos, sin, first16), kmax)
    put_t(vt_ref, 4, chunk(17))
    kmx_ref[...] = kmax


def _proj_even(xt, modl, w_in, cos, sin, g2, hm):
    t = xt.shape[0]
    nb = t // ROW_BLK
    return pl.pallas_call(
        _proj_even_kernel,
        grid=(nb,),
        in_specs=[pl.BlockSpec((ROW_BLK, D_MODEL), lambda i: (i, 0)),
                  _mod_spec(nb - 1),
                  pl.BlockSpec((D_MODEL, EVEN_IN), lambda i: (0, 0)),
                  pl.BlockSpec((ROW_BLK, LANES), lambda i: (i, 0)),
                  pl.BlockSpec((ROW_BLK, LANES), lambda i: (i, 0)),
                  pl.BlockSpec((2, LANES), lambda i: (0, 0)),
                  pl.BlockSpec((LANES, LANES), lambda i: (0, 0))],
        out_specs=[pl.BlockSpec((N_MAPS, LANES, ROW_BLK), lambda i: (0, 0, i)),
                   pl.BlockSpec((N_KV_CHUNKS, ROW_BLK, LANES), lambda i: (0, i, 0)),
                   pl.BlockSpec((N_KV_CHUNKS, LANES, ROW_BLK), lambda i: (0, 0, i)),
                   pl.BlockSpec((N_MAPS, ROW_BLK), lambda i: (0, i)),
                   pl.BlockSpec((None, 8, LANES), lambda i: (i, 0, 0))],
        out_shape=[jax.ShapeDtypeStruct((N_MAPS, LANES, t), BF16),
                   jax.ShapeDtypeStruct((N_KV_CHUNKS, t, LANES), BF16),
                   jax.ShapeDtypeStruct((N_KV_CHUNKS, LANES, t), BF16),
                   jax.ShapeDtypeStruct((N_MAPS, t), F32),
                   jax.ShapeDtypeStruct((nb, 8, LANES), F32)],
        compiler_params=_cparams(1),
        name="proj_even",
    )(xt, modl, w_in, cos, sin, g2, hm)


def _kv_chunk(mm):
    return mm % A_HEADS if mm < 2 * A_HEADS else A_HEADS


def _attn_even_kernel(kmax_ref, lam_ref, subg_ref, qn_ref, qt_ref, k_ref, vt_ref, o_ref, m_sc, l_sc, acc_sc, *,
                      nkv, lam_init):
    j = pl.program_id(1)

    @pl.when(j == 0)
    def _():
        m_sc[...] = jnp.full(m_sc.shape, NEG, F32)
        l_sc[...] = jnp.zeros(l_sc.shape, F32)
        acc_sc[...] = jnp.zeros(acc_sc.shape, F32)

    gap = None
    for mm in range(N_MAPS):
        ub = qn_ref[mm:mm + 1, :] * (kmax_ref[j, _kv_chunk(mm)] * NORM_SLACK)
        g = ub - m_sc[mm]
        gap = g if gap is None else jnp.maximum(gap, g)
    keep_ref = jnp.max(gap) < SAFE_LOG2_GAP

    def scores(mm):
        return jnp.dot(k_ref[_kv_chunk(mm)], qt_ref[mm], preferred_element_type=F32)

    def accumulate_keep(mm, s):
        p = jnp.exp2(s - m_sc[mm])
        l_sc[mm] = l_sc[mm] + jnp.sum(p, axis=0, keepdims=True)
        acc_sc[mm] = acc_sc[mm] + jnp.dot(vt_ref[_kv_chunk(mm)], p.astype(BF16), preferred_element_type=F32)

    def accumulate_move(mm, s):
        m_prev = m_sc[mm]
        m_new = jnp.maximum(m_prev, jnp.max(s, axis=0, keepdims=True))
        alpha = jnp.exp2(m_prev - m_new)
        p = jnp.exp2(s - m_new)
        l_sc[mm] = alpha * l_sc[mm] + jnp.sum(p, axis=0, keepdims=True)
        pv = jnp.dot(vt_ref[_kv_chunk(mm)], p.astype(BF16), preferred_element_type=F32)
        acc_sc[mm] = alpha * acc_sc[mm] + pv
        m_sc[mm] = m_new

    def all_maps(accumulate):
        s_next = scores(0)
        for mm in range(N_MAPS):
            s_cur = s_next
            if mm + 1 < N_MAPS:
                s_next = scores(mm + 1)
            accumulate(mm, s_cur)

    @pl.when(keep_ref)
    def _():
        all_maps(accumulate_keep)

    @pl.when(jnp.logical_not(keep_ref))
    def _():
        all_maps(accumulate_move)

    @pl.when(j == nkv - 1)
    def _():
        lamv = lam_ref[...]
        e1 = jnp.exp(jnp.sum(lamv[0:1, :] * lamv[1:2, :], axis=1, keepdims=True))
        e2 = jnp.exp(jnp.sum(lamv[2:3, :] * lamv[3:4, :], axis=1, keepdims=True))
        lam = e1 - e2 + lam_init
        subg = subg_ref[...]
        for hd in range(A_HEADS):
            o = acc_sc[hd] / l_sc[hd] - lam * (acc_sc[A_HEADS + hd] / l_sc[A_HEADS + hd])
            ms = jnp.mean(o * o, axis=0, keepdims=True)
            o = o * lax.rsqrt(ms + LN_EPS) * subg * (1.0 - lam_init)
            o_ref[:, hd * LANES:(hd + 1) * LANES] = o.T.astype(BF16)
        for pr in range(B_HEADS // 2):
            oa = acc_sc[8 + 2 * pr] / l_sc[8 + 2 * pr]
            ob = acc_sc[8 + 2 * pr + 1] / l_sc[8 + 2 * pr + 1]
            r0 = 0 if pr < 2 else HEAD_DIM
            o = jnp.concatenate([oa[r0:r0 + HEAD_DIM, :], ob[r0:r0 + HEAD_DIM, :]], axis=0)
            o_ref[:, (A_HEADS + pr) * LANES:(A_HEADS + pr + 1) * LANES] = o.T.astype(BF16)


def _attn_even(qt, k, vt, qn, kmx, lamf, subg, *, q_row0, n_q, k_row0, n_k, bq, bk, lam_init):
    assert q_row0 % bq == 0 and n_q % bq == 0 and k_row0 % bk == 0 and n_k % bk == 0 and bk % ROW_BLK == 0
    nq = n_q // bq
    nkv = n_k // bk
    qoff = q_row0 // bq
    koff = k_row0 // bk
    kmax = kmx[k_row0 // ROW_BLK:(k_row0 + n_k) // ROW_BLK, 0, 0:8]
    kmax = jnp.sqrt(jnp.max(kmax.reshape(nkv, bk // ROW_BLK, 8), axis=1))
    kern = functools.partial(_attn_even_kernel, nkv=nkv, lam_init=lam_init)
    return pl.pallas_call(
        kern,
        grid=(nq, nkv),
        in_specs=[pl.BlockSpec(memory_space=pltpu.SMEM),
                  pl.BlockSpec((4, HEAD_DIM), lambda i, j: (0, 0)),
                  pl.BlockSpec((LANES, 1), lambda i, j: (0, 0)),
                  pl.BlockSpec((N_MAPS, bq), lambda i, j: (0, i + qoff)),
                  pl.BlockSpec((N_MAPS, LANES, bq), lambda i, j: (0, 0, i + qoff)),
                  pl.BlockSpec((N_KV_CHUNKS, bk, LANES), lambda i, j: (0, j + koff, 0)),
                  pl.BlockSpec((N_KV_CHUNKS, LANES, bk), lambda i, j: (0, 0, j + koff))],
        out_specs=pl.BlockSpec((bq, D_MODEL), lambda i, j: (i, 0)),
        out_shape=jax.ShapeDtypeStruct((n_q, D_MODEL), BF16),
        scratch_shapes=[pltpu.VMEM((N_MAPS, 1, bq), F32), pltpu.VMEM((N_MAPS, 1, bq), F32),
                        pltpu.VMEM((N_MAPS, LANES, bq), F32)],
        compiler_params=_cparams(2),
        name="attn_even",
    )(kmax, lamf, subg, qn, qt, k, vt)


def _proj_odd_kernel(x_ref, mod_ref, w_ref, q_ref, k_ref, v_ref):
    x = x_ref[...]
    h = (x * (1.0 + mod_ref[1:2, :]) + mod_ref[0:1, :]).astype(BF16)
    d = D_MODEL
    q_ref[...] = (jnp.dot(h, w_ref[:, 0:d], preferred_element_type=F32) * (HEAD_DIM ** -0.5)).astype(BF16)
    k_ref[...] = jnp.dot(h, w_ref[:, d:2 * d], preferred_element_type=F32).astype(BF16)
    v_ref[...] = jnp.dot(h, w_ref[:, 2 * d:3 * d], preferred_element_type=F32).astype(BF16)


def _proj_odd(xt, modl, w_in):
    t = xt.shape[0]
    d = D_MODEL
    spec = pl.BlockSpec((ROW_BLK, d), lambda i: (i, 0))
    return pl.pallas_call(
        _proj_odd_kernel,
        grid=(t // ROW_BLK,),
        in_specs=[spec, _mod_spec(t // ROW_BLK - 1), pl.BlockSpec((d, 3 * d), lambda i: (0, 0))],
        out_specs=[spec, spec, spec],
        out_shape=[jax.ShapeDtypeStruct((t, d), BF16)] * 3,
        compiler_params=_cparams(1),
        name="proj_odd",
    )(xt, modl, w_in)


def _softmax_av(parts):
    m = None
    for s, _ in parts:
        sm = jnp.max(s, axis=1, keepdims=True)
        m = sm if m is None else jnp.maximum(m, sm)
    l = None
    o = None
    for s, v in parts:
        p = jnp.exp(s - m)
        ps = jnp.sum(p, axis=1, keepdims=True)
        pv = jnp.dot(p.astype(BF16), v, preferred_element_type=F32)
        l = ps if l is None else l + ps
        o = pv if o is None else o + pv
    return o / l


def _natten_kernel(q_ref, k_ref, v_ref, bm_ref, o_ref, *, rows, nqb):
    n = rows * GRID_W
    nband = BAND_ROWS * GRID_W
    kc = k_ref[n:n + CTX_LEN, :]
    vc = v_ref[n:n + CTX_LEN, :]
    lane = lax.broadcasted_iota(jnp.int32, (NA_QBLK, LANES), 1)
    lo = lane < HEAD_DIM
    work = []
    for sub in range(NA_STEP_BLKS):
        b = pl.program_id(1) * NA_STEP_BLKS + sub
        band = jnp.clip(2 * b - NA_WIN_H // 2, 0, rows - BAND_ROWS)
        typ = jnp.where(b < 2, b, jnp.where(b >= nqb - 2, b - (nqb - 5), 2))
        start = pl.multiple_of(band * GRID_W, GRID_W)
        kb = k_ref[pl.ds(start, nband), :]
        vb = v_ref[pl.ds(start, nband), :]
        q = q_ref[sub * NA_QBLK:(sub + 1) * NA_QBLK, :]
        zero = jnp.zeros_like(q)
        qs = jnp.concatenate([jnp.where(lo, q, zero), jnp.where(lo, zero, q)], axis=0)
        work.append((_dot_nt(qs, kb), _dot_nt(qs, kc), typ, vb))
    for sub, (s_nb, s_cx, typ, vb) in enumerate(work):
        bias = bm_ref[typ].reshape(2 * NA_QBLK, nband)
        o = _softmax_av([(s_nb + bias, vb), (s_cx, vc)])
        o_ref[sub * NA_QBLK:(sub + 1) * NA_QBLK, :] = jnp.where(lo, o[:NA_QBLK], o[NA_QBLK:]).astype(BF16)


def _natten(q, k, v, bm, n):
    t = q.shape[0]
    rows = n // GRID_W
    nqb = n // NA_QBLK
    assert nqb % NA_STEP_BLKS == 0 and nqb >= 5
    npair = C_HEADS // 2
    step_rows = NA_STEP_BLKS * NA_QBLK
    kern = functools.partial(_natten_kernel, rows=rows, nqb=nqb)
    return pl.pallas_call(
        kern,
        grid=(npair, nqb // NA_STEP_BLKS),
        in_specs=[pl.BlockSpec((step_rows, LANES), lambda j, b: (b, j)),
                  pl.BlockSpec((t, LANES), lambda j, b: (0, j)),
                  pl.BlockSpec((t, LANES), lambda j, b: (0, j)),
                  pl.BlockSpec((5, 2, NA_QBLK, BAND_ROWS * GRID_W), lambda j, b: (0, j, 0, 0))],
        out_specs=pl.BlockSpec((step_rows, LANES), lambda j, b: (b, j)),
        out_shape=jax.ShapeDtypeStruct((n, D_MODEL), BF16),
        compiler_params=_cparams(2),
        name="natten",
    )(q, k, v, bm)


def _ctx_attn_kernel(q_ref, k_ref, v_ref, o_ref):
    q = q_ref[...]
    k = k_ref[...]
    v = v_ref[...]
    lane = lax.broadcasted_iota(jnp.int32, q.shape, 1)
    lo = lane < HEAD_DIM
    zero = jnp.zeros_like(q)
    outs = []
    for half in range(2):
        qh = jnp.where(lo, q, zero) if half == 0 else jnp.where(lo, zero, q)
        outs.append(_softmax_av([(_dot_nt(qh, k), v)]))
    o_ref[...] = jnp.where(lo, outs[0], outs[1]).astype(BF16)


def _ctx_attn(q, k, v):
    ctx_blk = q.shape[0] // CTX_LEN - 1
    spec = pl.BlockSpec((CTX_LEN, LANES), lambda j: (ctx_blk, j))
    return pl.pallas_call(
        _ctx_attn_kernel,
        grid=(C_HEADS // 2,),
        in_specs=[spec, spec, spec],
        out_specs=pl.BlockSpec((CTX_LEN, LANES), lambda j: (0, j)),
        out_shape=jax.ShapeDtypeStruct((CTX_LEN, D_MODEL), BF16),
        compiler_params=_cparams(1),
        name="ctx_attn",
    )(q, k, v)


def _natten_bias_tables(rpb, n):
    rows = n // GRID_W
    nb = rows // 2
    c = np.arange(GRID_W)
    sel = (np.arange(RPB_W)[:, None, None] == (c[None, None, :] - c[None, :, None] + NA_WIN_W - 1))
    toep = jnp.einsum("hdm,mck->hdck", rpb, jnp.asarray(sel, F32), precision=HIGHEST)
    cs = np.clip(c - NA_WIN_W // 2, 0, GRID_W - NA_WIN_W)
    kc = np.arange(GRID_W)
    col_ok = (kc[None, :] >= cs[:, None]) & (kc[None, :] < cs[:, None] + NA_WIN_W)
    colmask = jnp.asarray(np.where(col_ok, 0.0, NEG), F32)
    negblk = jnp.full((C_HEADS, GRID_W, GRID_W), NEG, F32)
    tables = []
    for b in (0, 1, 2, nb - 2, nb - 1):
        band = int(np.clip(2 * b - NA_WIN_H // 2, 0, rows - BAND_ROWS))
        qrows = []
        for qr in range(2):
            r = 2 * b + qr
            rs = int(np.clip(r - NA_WIN_H // 2, 0, rows - NA_WIN_H))
            blks = []
            for i in range(BAND_ROWS):
                kr = band + i
                if rs <= kr < rs + NA_WIN_H:
                    blks.append(toep[:, kr - r + NA_WIN_H - 1] + colmask[None])
                else:
                    blks.append(negblk)
            qrows.append(jnp.concatenate(blks, axis=2))
        tables.append(jnp.concatenate(qrows, axis=1))
    return jnp.stack(tables, axis=0)


def _post_attn_kernel(*refs, with_ctx, ctx_blk):
    if with_ctx:
        octx_ref, refs = refs[0], refs[1:]
    olat_ref, x_ref, mod_ref, wout_ref, lng_ref, lnb_ref, rwh_ref, rwl_ref, x1_ref, h2_ref, r_ref = refs
    o = olat_ref[...]
    if with_ctx:
        o = jnp.where(pl.program_id(0) == ctx_blk, octx_ref[...], o)
    mix = jnp.dot(o, wout_ref[...], preferred_element_type=F32)
    x1 = _layer_norm(DN_ALPHA * x_ref[...] + mod_ref[2:3, :] * mix, lng_ref[0:1, :], lnb_ref[0:1, :])
    x1_ref[...] = x1
    h2 = x1 * (1.0 + mod_ref[4:5, :]) + mod_ref[3:4, :]
    h2_ref[...] = h2
    h_hi = h2.astype(BF16)
    h_lo = (h2 - h_hi.astype(F32)).astype(BF16)
    logits = (jnp.dot(h_hi, rwh_ref[...], preferred_element_type=F32)
              + jnp.dot(h_hi, rwl_ref[...], preferred_element_type=F32)
              + jnp.dot(h_lo, rwh_ref[...], preferred_element_type=F32))
    lane = lax.broadcasted_iota(jnp.int32, logits.shape, 1).astype(F32)
    big = float(LANES)
    lg = jnp.where(lane < N_GROUPS, logits, NEG)
    mg = jnp.max(lg, axis=1, keepdims=True)
    gstar = jnp.min(jnp.where(lg == mg, lane, big), axis=1, keepdims=True)
    p_top = 1.0 / jnp.sum(jnp.exp(lg - mg), axis=1, keepdims=True)
    e_lo = N_GROUPS + EXPERTS_PER_GROUP * gstar
    le = jnp.where((lane >= e_lo) & (lane < e_lo + EXPERTS_PER_GROUP), logits, NEG)
    v1 = jnp.max(le, axis=1, keepdims=True)
    i1 = jnp.min(jnp.where(le == v1, lane, big), axis=1, keepdims=True)
    le2 = jnp.where(lane == i1, NEG, le)
    v2 = jnp.max(le2, axis=1, keepdims=True)
    i2 = jnp.min(jnp.where(le2 == v2, lane, big), axis=1, keepdims=True)
    e2 = jnp.exp(v2 - v1)
    w1 = p_top / (1.0 + e2)
    w2 = p_top * e2 / (1.0 + e2)
    r = jnp.where(lane == 0.0, i1 - N_GROUPS, 0.0)
    r = jnp.where(lane == 1.0, i2 - N_GROUPS, r)
    r = jnp.where(lane == 2.0, w1, r)
    r = jnp.where(lane == 3.0, w2, r)
    r_ref[...] = r


def _post_attn(o_ctx, o_lat, xt, modl, w_out, lng, lnb, rw, *, with_ctx):
    t = xt.shape[0]
    d = D_MODEL
    ctx_blk = (t - CTX_LEN) // ROW_BLK
    nblk = ctx_blk + 1 if with_ctx else ctx_blk
    row = pl.BlockSpec((ROW_BLK, d), lambda i: (i, 0))
    in_specs = [pl.BlockSpec((ROW_BLK, d), lambda i: (jnp.minimum(i, ctx_blk - 1), 0)),
                row, _mod_spec(ctx_blk),
                pl.BlockSpec((d, d), lambda i: (0, 0)),
                pl.BlockSpec((2, d), lambda i: (0, 0)),
                pl.BlockSpec((2, d), lambda i: (0, 0)),
                pl.BlockSpec((d, LANES), lambda i: (0, 0)),
                pl.BlockSpec((d, LANES), lambda i: (0, 0))]
    rw_hi = rw.astype(BF16)
    rw_lo = (rw - rw_hi.astype(F32)).astype(BF16)
    args = [o_lat, xt, modl, w_out, lng, lnb, rw_hi, rw_lo]
    if with_ctx:
        in_specs = [pl.BlockSpec((ROW_BLK, d), lambda i: (0, 0))] + in_specs
        args = [o_ctx] + args
    kern = functools.partial(_post_attn_kernel, with_ctx=with_ctx, ctx_blk=ctx_blk)
    return pl.pallas_call(
        kern,
        grid=(nblk,),
        in_specs=in_specs,
        out_specs=[row, row, pl.BlockSpec((ROW_BLK, LANES), lambda i: (i, 0))],
        out_shape=[jax.ShapeDtypeStruct((nblk * ROW_BLK, d), F32),
                   jax.ShapeDtypeStruct((nblk * ROW_BLK, d), F32),
                   jax.ShapeDtypeStruct((nblk * ROW_BLK, LANES), F32)],
        compiler_params=_cparams(1),
        name="post_attn",
    )(*args)


def _row_copies(n_rows, make):
    def start(r, c):
        for kk in range(2):
            make(r, kk).start(priority=kk)
        return c

    def wait(r, c):
        for kk in range(2):
            make(r, kk).wait()
        return c

    for r in range(n_rows):
        start(r, 0)
    lax.fori_loop(0, n_rows, wait, 0, unroll=ROW_COPY_UNROLL)


def _dispatch_kernel(dest_ref, h_ref, xs_in_ref, xs_ref, sem):
    del xs_in_ref

    def make(r, kk):
        dst = dest_ref[0, 0, 2 * r + kk]
        return pltpu.make_async_copy(h_ref.at[pl.ds(r, 1)], xs_ref.at[pl.ds(dst, 1)], sem)

    _row_copies(h_ref.shape[0], make)


def _dispatch(h2, dest3, xs0):
    nblk = dest3.shape[0]
    return pl.pallas_call(
        _dispatch_kernel,
        grid=(nblk,),
        in_specs=[pl.BlockSpec((1, 1, 2 * ROW_BLK), lambda i: (i, 0, 0), memory_space=pltpu.SMEM),
                  pl.BlockSpec((ROW_BLK, D_MODEL), lambda i: (i, 0)),
                  pl.BlockSpec(memory_space=pl.ANY)],
        out_specs=pl.BlockSpec(memory_space=pl.ANY),
        out_shape=jax.ShapeDtypeStruct(xs0.shape, xs0.dtype),
        scratch_shapes=[pltpu.SemaphoreType.DMA(())],
        input_output_aliases={2: 0},
        compiler_params=pltpu.CompilerParams(dimension_semantics=("arbitrary",),
                                             vmem_limit_bytes=VMEM_LIMIT, has_side_effects=True),
        name="moe_dispatch",
    )(dest3, h2, xs0)


def _experts_kernel(bexp_ref, nused_ref, xs_ref, w1_ref, w3_ref, w2_ref, y_ref):
    del bexp_ref
    b = pl.program_id(0)

    @pl.when(b < nused_ref[0])
    def _():
        x = xs_ref[...].astype(BF16)
        h1 = jnp.dot(x, w1_ref[...].astype(BF16), preferred_element_type=F32)
        h3 = jnp.dot(x, w3_ref[...].astype(BF16), preferred_element_type=F32)
        a = (_silu(h1) * h3).astype(BF16)
        y_ref[...] = jnp.dot(a, w2_ref[...].astype(BF16), preferred_element_type=F32)

    @pl.when(b >= nused_ref[0])
    def _():
        y_ref[...] = jnp.zeros(y_ref.shape, F32)


def _experts(block_exp, nused, xs, w1, w3, w2, layer):
    n_rows = xs.shape[0]
    d = D_MODEL
    grid_spec = pltpu.PrefetchScalarGridSpec(
        num_scalar_prefetch=2,
        grid=(n_rows // MOE_BLK,),
        in_specs=[pl.BlockSpec((MOE_BLK, d), lambda b, be, nu: (jnp.minimum(b, nu[0] - 1), 0)),
                  pl.BlockSpec((None, None, d, D_EXPERT), lambda b, be, nu: (layer, be[b], 0, 0)),
                  pl.BlockSpec((None, None, d, D_EXPERT), lambda b, be, nu: (layer, be[b], 0, 0)),
                  pl.BlockSpec((None, None, D_EXPERT, d), lambda b, be, nu: (layer, be[b], 0, 0))],
        out_specs=pl.BlockSpec((MOE_BLK, d), lambda b, be, nu: (b, 0)),
    )
    return pl.pallas_call(
        _experts_kernel,
        grid_spec=grid_spec,
        out_shape=jax.ShapeDtypeStruct((n_rows, d), F32),
        compiler_params=_cparams(1),
        name="moe_experts",
    )(block_exp, nused, xs, w1, w3, w2)


def _combine_kernel(dcur_ref, dnext_ref, x1_ref, r_ref, mod_ref, lng_ref, lnb_ref, y_ref, o_ref, ybuf, sem):
    i = pl.program_id(0)
    n_rows = x1_ref.shape[0]

    def gather(d_ref, dst_slot, r, kk):
        src = d_ref[0, 0, 2 * r + kk]
        return pltpu.make_async_copy(y_ref.at[pl.ds(src, 1)], ybuf.at[dst_slot, kk, pl.ds(r, 1)], sem.at[dst_slot])

    def start_rows(d_ref, dst_slot, r0, count):
        for rr in range(count):
            for kk in range(2):
                gather(d_ref, dst_slot, r0 + rr, kk).start(priority=kk)

    def wait_all(dst_slot):
        def body(r, c):
            for kk in range(2):
                gather(dcur_ref, dst_slot, r, kk).wait()
            return c
        lax.fori_loop(0, n_rows, body, 0, unroll=ROW_COPY_UNROLL)

    @pl.when(i == 0)
    def _():
        def body(g, c):
            start_rows(dcur_ref, 0, g * ROW_COPY_UNROLL, ROW_COPY_UNROLL)
            return c
        lax.fori_loop(0, n_rows // ROW_COPY_UNROLL, body, 0)

    g2 = mod_ref[5:6, :]
    lng = lng_ref[1:2, :]
    lnb = lnb_ref[1:2, :]

    def step(slot):
        other = 1 - slot

        start_rows(dnext_ref, other, 0, n_rows)
        wait_all(slot)
        ymix = ybuf[slot, 0] * r_ref[:, 2:3] + ybuf[slot, 1] * r_ref[:, 3:4]
        o_ref[...] = _layer_norm(DN_ALPHA * x1_ref[...] + g2 * ymix, lng, lnb)

        @pl.when(i == pl.num_programs(0) - 1)
        def _():
            wait_all(other)

    @pl.when(lax.rem(i, 2) == 0)
    def _():
        step(0)

    @pl.when(lax.rem(i, 2) == 1)
    def _():
        step(1)


def _combine(dest3, x1, r, modl, lng, lnb, y, *, ctx_blk):
    nblk = dest3.shape[0]
    d = D_MODEL
    return pl.pallas_call(
        _combine_kernel,
        grid=(nblk,),
        in_specs=[pl.BlockSpec((1, 1, 2 * ROW_BLK), lambda i: (i, 0, 0), memory_space=pltpu.SMEM),
                  pl.BlockSpec((1, 1, 2 * ROW_BLK), lambda i: (jnp.minimum(i + 1, nblk - 1), 0, 0),
                               memory_space=pltpu.SMEM),
                  pl.BlockSpec((ROW_BLK, d), lambda i: (i, 0)),
                  pl.BlockSpec((ROW_BLK, LANES), lambda i: (i, 0)),
                  _mod_spec(ctx_blk),
                  pl.BlockSpec((2, d), lambda i: (0, 0)),
                  pl.BlockSpec((2, d), lambda i: (0, 0)),
                  pl.BlockSpec(memory_space=pl.ANY)],
        out_specs=pl.BlockSpec((ROW_BLK, d), lambda i: (i, 0)),
        out_shape=jax.ShapeDtypeStruct((nblk * ROW_BLK, d), F32),
        scratch_shapes=[pltpu.VMEM((2, 2, ROW_BLK, d), F32), pltpu.SemaphoreType.DMA((2,))],
        compiler_params=_cparams(1),
        name="moe_combine",
    )(dest3, dest3, x1, r, modl, lng, lnb, y)


def _dispatch_plan(r):
    ntok = r.shape[0]
    ef = r[:, 0:2].astype(jnp.int32).reshape(-1)
    na = 2 * ntok
    sub = 256
    oh = (ef[:, None] == jnp.arange(N_EXPERTS, dtype=jnp.int32)[None, :]).astype(F32)
    ohb = oh.reshape(na // sub, sub, N_EXPERTS)
    tril = jnp.tril(jnp.ones((sub, sub), F32))
    within = jnp.einsum("ij,bjk->bik", tril, ohb)
    tot = within[:, -1, :]
    before = jnp.cumsum(tot, axis=0) - tot
    rank = jnp.sum((within + before[:, None, :]) * ohb, axis=2).reshape(na) - 1.0
    counts = jnp.sum(tot, axis=0).astype(jnp.int32)
    padded = (counts + MOE_BLK - 1) // MOE_BLK * MOE_BLK
    pad_ends = jnp.cumsum(padded)
    pad_starts = pad_ends - padded
    dest = jnp.sum(oh * pad_starts.astype(F32)[None, :], axis=1) + rank
    dest = dest.astype(jnp.int32)
    n_blocks = na // MOE_BLK + N_EXPERTS
    blk_start = jnp.arange(n_blocks, dtype=jnp.int32) * MOE_BLK
    block_exp = jnp.sum((pad_ends[None, :] <= blk_start[:, None]).astype(jnp.int32), axis=1)
    block_exp = jnp.minimum(block_exp, N_EXPERTS - 1)
    nused = (pad_ends[-1:] // MOE_BLK).astype(jnp.int32)
    return dest.reshape(ntok // ROW_BLK, 1, 2 * ROW_BLK), block_exp, nused, n_blocks


def _moe(h2, x1, r, modl, lng, lnb, w1, w3, w2, *, layer, ctx_blk):
    dest3, block_exp, nused, n_blocks = _dispatch_plan(r)
    xs0 = jnp.zeros((n_blocks * MOE_BLK, D_MODEL), F32)
    xs = _dispatch(h2, dest3, xs0)
    y = _experts(block_exp, nused, xs, w1, w3, w2, layer)
    return _combine(dest3, x1, r, modl, lng, lnb, y, ctx_blk=ctx_blk)


def _rope_tables(n):
    tkn = jnp.arange(n, dtype=jnp.int32)
    row = (tkn // GRID_W).astype(F32)
    col = (tkn % GRID_W).astype(F32)
    half = HEAD_DIM // 2
    inv = ROPE_THETA ** (-jnp.arange(0, half, 2, dtype=F32) / half)
    ang_r = row[:, None] * inv[None, :]
    ang_c = col[:, None] * inv[None, :]
    ang = jnp.concatenate([ang_r, ang_r, ang_c, ang_c] * 2, axis=-1)
    cos = jnp.concatenate([jnp.cos(ang), jnp.ones((CTX_LEN, LANES), F32)], axis=0)
    sin = jnp.concatenate([jnp.sin(ang), jnp.zeros((CTX_LEN, LANES), F32)], axis=0)
    return cos, sin


def _kv_block(t):
    for cand in (1280, 1024, 768, 512, 256):
        if t % cand == 0:
            return cand
    raise ValueError(f"unsupported token count {t}")


def kernel(x, c, ctx, c_ctx, w_mod, b_mod, ln_g, ln_b, w_in_even, w_out_even, diff_lam, diff_subln_g,
           gqa_qk_g, w_in_odd, w_out_odd, na_rpb, router_g, router_e, w1, w3, w2):
    b, n, d = x.shape
    assert b == 1 and d == D_MODEL and ctx.shape == (1, CTX_LEN, D_MODEL)
    assert n % ROW_BLK == 0 and n // GRID_W >= BAND_ROWS and CTX_LEN == ROW_BLK
    t = n + CTX_LEN
    ctx_blk = n // ROW_BLK
    xt = jnp.concatenate([x[0], ctx[0]], axis=0)
    cvec = jnp.zeros((8, d), F32).at[0].set(c[0]).at[1].set(c_ctx)
    mod = _modulation(cvec, w_mod, b_mod)[:, 0:2, :].reshape(DEPTH, 2, 6, d)
    cos, sin = _rope_tables(n)
    hm = jnp.asarray(np.kron(np.eye(2), np.full((HEAD_DIM, HEAD_DIM), 1.0 / HEAD_DIM)), F32)
    rw = jnp.zeros((DEPTH, d, LANES), F32)
    rw = rw.at[:, :, 0:N_GROUPS].set(router_g).at[:, :, N_GROUPS:N_GROUPS + N_EXPERTS].set(router_e)
    bk = _kv_block(t)
    bq = ATTN_BQ if n % ATTN_BQ == 0 else ROW_BLK
    out = None
    for l in range(DEPTH):
        last = l == DEPTH - 1
        i = l // 2
        modl = mod[l]
        if l % 2 == 0:
            lam_init = 0.8 - 0.6 * math.exp(-0.3 * l)
            q, k, v, qn, kmx = _proj_even(xt, modl, w_in_even[i].astype(BF16), cos, sin,
                                          jnp.tile(gqa_qk_g[i], (1, 2)), hm)
            subg = diff_subln_g[i].reshape(LANES, 1)
            o_lat = _attn_even(q, k, v, qn, kmx, diff_lam[i], subg, q_row0=0, n_q=n, k_row0=0, n_k=t,
                               bq=bq, bk=bk, lam_init=lam_init)
            o_ctx = None if last else _attn_even(q, k, v, qn, kmx, diff_lam[i], subg, q_row0=n, n_q=CTX_LEN,
                                                 k_row0=n, n_k=CTX_LEN, bq=CTX_LEN, bk=CTX_LEN,
                                                 lam_init=lam_init)
            w_out = w_out_even[i]
        else:
            q, k, v = _proj_odd(xt, modl, w_in_odd[i].astype(BF16))
            bm = _natten_bias_tables(na_rpb[i], n)
            o_lat = _natten(q, k, v, bm, n)
            o_ctx = None if last else _ctx_attn(q, k, v)
            w_out = w_out_odd[i]
        x1, h2, r = _post_attn(o_ctx, o_lat, xt, modl, w_out.astype(BF16), ln_g[l], ln_b[l], rw[l],
                               with_ctx=not last)
        xt = _moe(h2, x1, r, modl, ln_g[l], ln_b[l], w1, w3, w2, layer=l, ctx_blk=ctx_blk)
        out = xt
    return out.reshape(1, n, d)
```

```python
import functools
import math

import numpy as np
import jax
import jax.numpy as jnp
from jax import lax
from jax.experimental import pallas as pl
from jax.experimental.pallas import tpu as pltpu

D_MODEL = 1024
DEPTH = 4
GRID_W = 64
CTX_LEN = 256
HEAD_DIM = 64
A_HEADS = 4
B_HEADS = 8
B_KV_HEADS = 2
C_HEADS = D_MODEL // HEAD_DIM
NA_WIN_H = 8
NA_WIN_W = 16
RPB_H = 2 * NA_WIN_H - 1
RPB_W = 2 * NA_WIN_W - 1
N_GROUPS = 4
EXPERTS_PER_GROUP = 8
N_EXPERTS = N_GROUPS * EXPERTS_PER_GROUP
D_EXPERT = 512
ROPE_THETA = 10000.0
DN_ALPHA = (2.0 * DEPTH) ** 0.25
EVEN_IN = 2304
LN_EPS = 1e-6

LANES = 128
ROW_BLK = 256
MOE_BLK = 256
ROW_COPY_UNROLL = 8
N_MAPS = 16
N_KV_CHUNKS = 5
ATTN_BQ = 512
LOG2E = math.log2(math.e)
SAFE_LOG2_GAP = 80.0
NORM_SLACK = 1.001
BAND_ROWS = 10
NA_QBLK = 2 * GRID_W
NA_STEP_BLKS = 4
NEG = -1e30
VMEM_LIMIT = 48 * 1024 * 1024

F32 = jnp.float32
BF16 = jnp.bfloat16
HIGHEST = lax.Precision.HIGHEST


def _cparams(n_axes):
    return pltpu.CompilerParams(dimension_semantics=("arbitrary",) * n_axes,
                                vmem_limit_bytes=VMEM_LIMIT)


def _dot_nt(a, b):
    return lax.dot_general(a, b, (((1,), (1,)), ((), ())), preferred_element_type=F32)


def _layer_norm(v, g, b):
    mu = jnp.mean(v, axis=-1, keepdims=True)
    d = v - mu
    var = jnp.mean(d * d, axis=-1, keepdims=True)
    return d * lax.rsqrt(var + LN_EPS) * g + b


def _silu(v):
    return v / (1.0 + jnp.exp(-v))


def _mod_kernel(c_ref, w_ref, b_ref, o_ref):
    s = _silu(c_ref[...])
    o_ref[...] = jnp.dot(s, w_ref[...], preferred_element_type=F32, precision=HIGHEST) + b_ref[...]


def _modulation(cvec, w_mod, b_mod):
    d = D_MODEL
    return pl.pallas_call(
        _mod_kernel,
        grid=(DEPTH, 6),
        in_specs=[pl.BlockSpec((8, d), lambda l, j: (0, 0)),
                  pl.BlockSpec((None, d, d), lambda l, j: (l, 0, j)),
                  pl.BlockSpec((None, 1, d), lambda l, j: (l, 0, j))],
        out_specs=pl.BlockSpec((None, 8, d), lambda l, j: (l, 0, j)),
        out_shape=jax.ShapeDtypeStruct((DEPTH, 8, 6 * d), F32),
        compiler_params=_cparams(2),
        name="modulation",
    )(cvec, w_mod, b_mod.reshape(DEPTH, 1, 6 * d))


def _mod_spec(ctx_blk):
    return pl.BlockSpec((None, 6, D_MODEL), lambda i: (jnp.where(i == ctx_blk, 1, 0), 0, 0))


def _rope(z, cos, sin, first16):
    rot = jnp.where(first16, -pltpu.roll(z, LANES - 16, 1), pltpu.roll(z, 16, 1))
    return z * cos + rot * sin


def _proj_even_kernel(x_ref, mod_ref, w_ref, cos_ref, sin_ref, g_ref, hm_ref, qt_ref, k_ref, vt_ref, qn_ref, kmx_ref):
    x = x_ref[...]
    h = (x * (1.0 + mod_ref[1:2, :]) + mod_ref[0:1, :]).astype(BF16)
    y = jnp.dot(h, w_ref[...], preferred_element_type=F32)
    cos = cos_ref[...]
    sin = sin_ref[...]
    lane = lax.broadcasted_iota(jnp.int32, cos.shape, 1)
    first16 = (lane % 32) < 16
    lo = lane < HEAD_DIM
    hm = hm_ref[...]
    scale = HEAD_DIM ** -0.5 * LOG2E

    def chunk(c):
        return y[:, c * LANES:(c + 1) * LANES]

    def qk_norm(z, g):
        ms = jnp.dot(z * z, hm, preferred_element_type=F32, precision=HIGHEST)
        return z * lax.rsqrt(ms + LN_EPS) * g

    def put_t(ref, idx, val):
        ref[idx] = val.T.astype(BF16)

    def put_q(idx, val):
        vb = val.T.astype(BF16)
        qt_ref[idx] = vb
        vf = vb.astype(F32)
        qn_ref[idx:idx + 1, :] = jnp.sqrt(jnp.sum(vf * vf, axis=0, keepdims=True))

    kmax = jnp.zeros(kmx_ref.shape, F32)
    klane = lax.broadcasted_iota(jnp.int32, kmx_ref.shape, 1)

    def put_k(idx, val, kmax):
        kb = val.astype(BF16)
        k_ref[idx] = kb
        kf = kb.astype(F32)
        n2 = jnp.max(jnp.sum(kf * kf, axis=1, keepdims=True), axis=0, keepdims=True)
        return jnp.where(klane == idx, n2, kmax)

    for hd in range(A_HEADS):
        q = _rope(chunk(hd), cos, sin, first16) * scale
        put_q(hd, jnp.where(lo, q, 0.0))
        put_q(A_HEADS + hd, jnp.where(lo, 0.0, q))
        kmax = put_k(hd, _rope(chunk(4 + hd), cos, sin, first16), kmax)
        put_t(vt_ref, hd, chunk(8 + hd))
    gq = g_ref[0:1, :]
    gk = g_ref[1:2, :]
    for pr in range(B_HEADS // 2):
        q = _rope(qk_norm(chunk(12 + pr), gq), cos, sin, first16) * scale
        qs = pltpu.roll(q, HEAD_DIM, 1)
        if pr < 2:
            put_q(8 + 2 * pr, jnp.where(lo, q, 0.0))
            put_q(8 + 2 * pr + 1, jnp.where(lo, qs, 0.0))
        else:
            put_q(8 + 2 * pr, jnp.where(lo, 0.0, qs))
            put_q(8 + 2 * pr + 1, jnp.where(lo, 0.0, q))
    kmax = put_k(4, _rope(qk_norm(chunk(16), gk), cos, sin, first16), kmax)
    put_t(vt_ref, 4, chunk(17))
    kmx_ref[...] = kmax


def _proj_even(xt, modl, w_in, cos, sin, g2, hm):
    t = xt.shape[0]
    nb = t // ROW_BLK
    return pl.pallas_call(
        _proj_even_kernel,
        grid=(nb,),
        in_specs=[pl.BlockSpec((ROW_BLK, D_MODEL), lambda i: (i, 0)),
                  _mod_spec(nb - 1),
                  pl.BlockSpec((D_MODEL, EVEN_IN), lambda i: (0, 0)),
                  pl.BlockSpec((ROW_BLK, LANES), lambda i: (i, 0)),
                  pl.BlockSpec((ROW_BLK, LANES), lambda i: (i, 0)),
                  pl.BlockSpec((2, LANES), lambda i: (0, 0)),
                  pl.BlockSpec((LANES, LANES), lambda i: (0, 0))],
        out_specs=[pl.BlockSpec((N_MAPS, LANES, ROW_BLK), lambda i: (0, 0, i)),
                   pl.BlockSpec((N_KV_CHUNKS, ROW_BLK, LANES), lambda i: (0, i, 0)),
                   pl.BlockSpec((N_KV_CHUNKS, LANES, ROW_BLK), lambda i: (0, 0, i)),
                   pl.BlockSpec((N_MAPS, ROW_BLK), lambda i: (0, i)),
                   pl.BlockSpec((None, 8, LANES), lambda i: (i, 0, 0))],
        out_shape=[jax.ShapeDtypeStruct((N_MAPS, LANES, t), BF16),
                   jax.ShapeDtypeStruct((N_KV_CHUNKS, t, LANES), BF16),
                   jax.ShapeDtypeStruct((N_KV_CHUNKS, LANES, t), BF16),
                   jax.ShapeDtypeStruct((N_MAPS, t), F32),
                   jax.ShapeDtypeStruct((nb, 8, LANES), F32)],
        compiler_params=_cparams(1),
        name="proj_even",
    )(xt, modl, w_in, cos, sin, g2, hm)


def _kv_chunk(mm):
    return mm % A_HEADS if mm < 2 * A_HEADS else A_HEADS


def _attn_even_kernel(kmax_ref, lam_ref, subg_ref, qn_ref, qt_ref, k_ref, vt_ref, o_ref, m_sc, l_sc, acc_sc, mg_sc,
                      *, nkv, lam_init):
    j = pl.program_id(1)

    @pl.when(j == 0)
    def _():
        m_sc[...] = jnp.full(m_sc.shape, NEG, F32)
        mg_sc[...] = jnp.full(mg_sc.shape, NEG, F32)
        l_sc[...] = jnp.zeros(l_sc.shape, F32)
        acc_sc[...] = jnp.zeros(acc_sc.shape, F32)

    sub = lax.broadcasted_iota(jnp.int32, (N_MAPS, 1), 0)
    chunk_of_map = jnp.where(sub >= 2 * A_HEADS, A_HEADS, sub & (A_HEADS - 1))
    kvec = jnp.zeros((N_MAPS, 1), F32)
    for c in range(N_KV_CHUNKS):
        kvec = jnp.where(chunk_of_map == c, kmax_ref[j, c] * NORM_SLACK, kvec)
    keep_ref = jnp.max(qn_ref[...] * kvec - mg_sc[...]) < SAFE_LOG2_GAP

    def scores(mm):
        return jnp.dot(k_ref[_kv_chunk(mm)], qt_ref[mm], preferred_element_type=F32)

    def accumulate_keep(mm, s):
        p = jnp.exp2(s - m_sc[mm])
        l_sc[mm] = l_sc[mm] + jnp.sum(p, axis=0, keepdims=True)
        acc_sc[mm] = acc_sc[mm] + jnp.dot(vt_ref[_kv_chunk(mm)], p.astype(BF16), preferred_element_type=F32)

    def accumulate_move(mm, s):
        m_prev = m_sc[mm]
        m_new = jnp.maximum(m_prev, jnp.max(s, axis=0, keepdims=True))
        alpha = jnp.exp2(m_prev - m_new)
        p = jnp.exp2(s - m_new)
        l_sc[mm] = alpha * l_sc[mm] + jnp.sum(p, axis=0, keepdims=True)
        pv = jnp.dot(vt_ref[_kv_chunk(mm)], p.astype(BF16), preferred_element_type=F32)
        acc_sc[mm] = alpha * acc_sc[mm] + pv
        m_sc[mm] = m_new
        mg_sc[mm:mm + 1, :] = m_new

    def all_maps(accumulate):
        s_next = scores(0)
        for mm in range(N_MAPS):
            s_cur = s_next
            if mm + 1 < N_MAPS:
                s_next = scores(mm + 1)
            accumulate(mm, s_cur)

    @pl.when(keep_ref)
    def _():
        all_maps(accumulate_keep)

    @pl.when(jnp.logical_not(keep_ref))
    def _():
        all_maps(accumulate_move)

    @pl.when(j == nkv - 1)
    def _():
        lamv = lam_ref[...]
        e1 = jnp.exp(jnp.sum(lamv[0:1, :] * lamv[1:2, :], axis=1, keepdims=True))
        e2 = jnp.exp(jnp.sum(lamv[2:3, :] * lamv[3:4, :], axis=1, keepdims=True))
        lam = e1 - e2 + lam_init
        subg = subg_ref[...]
        for hd in range(A_HEADS):
            o = acc_sc[hd] / l_sc[hd] - lam * (acc_sc[A_HEADS + hd] / l_sc[A_HEADS + hd])
            ms = jnp.mean(o * o, axis=0, keepdims=True)
            o = o * lax.rsqrt(ms + LN_EPS) * subg * (1.0 - lam_init)
            o_ref[:, hd * LANES:(hd + 1) * LANES] = o.T.astype(BF16)
        for pr in range(B_HEADS // 2):
            oa = acc_sc[8 + 2 * pr] / l_sc[8 + 2 * pr]
            ob = acc_sc[8 + 2 * pr + 1] / l_sc[8 + 2 * pr + 1]
            r0 = 0 if pr < 2 else HEAD_DIM
            o = jnp.concatenate([oa[r0:r0 + HEAD_DIM, :], ob[r0:r0 + HEAD_DIM, :]], axis=0)
            o_ref[:, (A_HEADS + pr) * LANES:(A_HEADS + pr + 1) * LANES] = o.T.astype(BF16)


def _attn_even(qt, k, vt, qn, kmx, lamf, subg, *, q_row0, n_q, k_row0, n_k, bq, bk, lam_init):
    assert q_row0 % bq == 0 and n_q % bq == 0 and k_row0 % bk == 0 and n_k % bk == 0 and bk % ROW_BLK == 0
    nq = n_q // bq
    nkv = n_k // bk
    qoff = q_row0 // bq
    koff = k_row0 // bk
    kmax = kmx[k_row0 // ROW_BLK:(k_row0 + n_k) // ROW_BLK, 0, 0:8]
    kmax = jnp.sqrt(jnp.max(kmax.reshape(nkv, bk // ROW_BLK, 8), axis=1))
    kern = functools.partial(_attn_even_kernel, nkv=nkv, lam_init=lam_init)
    return pl.pallas_call(
        kern,
        grid=(nq, nkv),
        in_specs=[pl.BlockSpec(memory_space=pltpu.SMEM),
                  pl.BlockSpec((4, HEAD_DIM), lambda i, j: (0, 0)),
                  pl.BlockSpec((LANES, 1), lambda i, j: (0, 0)),
                  pl.BlockSpec((N_MAPS, bq), lambda i, j: (0, i + qoff)),
                  pl.BlockSpec((N_MAPS, LANES, bq), lambda i, j: (0, 0, i + qoff)),
                  pl.BlockSpec((N_KV_CHUNKS, bk, LANES), lambda i, j: (0, j + koff, 0)),
                  pl.BlockSpec((N_KV_CHUNKS, LANES, bk), lambda i, j: (0, 0, j + koff))],
        out_specs=pl.BlockSpec((bq, D_MODEL), lambda i, j: (i, 0)),
        out_shape=jax.ShapeDtypeStruct((n_q, D_MODEL), BF16),
        scratch_shapes=[pltpu.VMEM((N_MAPS, 1, bq), F32), pltpu.VMEM((N_MAPS, 1, bq), F32),
                        pltpu.VMEM((N_MAPS, LANES, bq), F32), pltpu.VMEM((N_MAPS, bq), F32)],
        compiler_params=_cparams(2),
        name="attn_even",
    )(kmax, lamf, subg, qn, qt, k, vt)


def _proj_odd_kernel(x_ref, mod_ref, w_ref, q_ref, k_ref, v_ref):
    x = x_ref[...]
    h = (x * (1.0 + mod_ref[1:2, :]) + mod_ref[0:1, :]).astype(BF16)
    d = D_MODEL
    q_ref[...] = (jnp.dot(h, w_ref[:, 0:d], preferred_element_type=F32) * (HEAD_DIM ** -0.5)).astype(BF16)
    k_ref[...] = jnp.dot(h, w_ref[:, d:2 * d], preferred_element_type=F32).astype(BF16)
    v_ref[...] = jnp.dot(h, w_ref[:, 2 * d:3 * d], preferred_element_type=F32).astype(BF16)


def _proj_odd(xt, modl, w_in):
    t = xt.shape[0]
    d = D_MODEL
    spec = pl.BlockSpec((ROW_BLK, d), lambda i: (i, 0))
    return pl.pallas_call(
        _proj_odd_kernel,
        grid=(t // ROW_BLK,),
        in_specs=[spec, _mod_spec(t // ROW_BLK - 1), pl.BlockSpec((d, 3 * d), lambda i: (0, 0))],
        out_specs=[spec, spec, spec],
        out_shape=[jax.ShapeDtypeStruct((t, d), BF16)] * 3,
        compiler_params=_cparams(1),
        name="proj_odd",
    )(xt, modl, w_in)


def _softmax_av(parts):
    m = None
    for s, _ in parts:
        sm = jnp.max(s, axis=1, keepdims=True)
        m = sm if m is None else jnp.maximum(m, sm)
    l = None
    o = None
    for s, v in parts:
        p = jnp.exp(s - m)
        ps = jnp.sum(p, axis=1, keepdims=True)
        pv = jnp.dot(p.astype(BF16), v, preferred_element_type=F32)
        l = ps if l is None else l + ps
        o = pv if o is None else o + pv
    return o / l


def _natten_kernel(q_ref, k_ref, v_ref, bm_ref, o_ref, *, rows, nqb):
    n = rows * GRID_W
    nband = BAND_ROWS * GRID_W
    kc = k_ref[n:n + CTX_LEN, :]
    vc = v_ref[n:n + CTX_LEN, :]
    lane = lax.broadcasted_iota(jnp.int32, (NA_QBLK, LANES), 1)
    lo = lane < HEAD_DIM
    def scores(sub):
        b = pl.program_id(1) * NA_STEP_BLKS + sub
        band = jnp.clip(2 * b - NA_WIN_H // 2, 0, rows - BAND_ROWS)
        typ = jnp.where(b < 2, b, jnp.where(b >= nqb - 2, b - (nqb - 5), 2))
        start = pl.multiple_of(band * GRID_W, GRID_W)
        kb = k_ref[pl.ds(start, nband), :]
        vb = v_ref[pl.ds(start, nband), :]
        q = q_ref[sub * NA_QBLK:(sub + 1) * NA_QBLK, :]
        zero = jnp.zeros_like(q)
        qs = jnp.concatenate([jnp.where(lo, q, zero), jnp.where(lo, zero, q)], axis=0)
        return _dot_nt(qs, kb), _dot_nt(qs, kc), typ, vb

    nxt = scores(0)
    for sub in range(NA_STEP_BLKS):
        s_nb, s_cx, typ, vb = nxt
        if sub + 1 < NA_STEP_BLKS:
            nxt = scores(sub + 1)
        bias = bm_ref[typ].reshape(2 * NA_QBLK, nband)
        o = _softmax_av([(s_nb + bias, vb), (s_cx, vc)])
        o_ref[sub * NA_QBLK:(sub + 1) * NA_QBLK, :] = jnp.where(lo, o[:NA_QBLK], o[NA_QBLK:]).astype(BF16)


def _natten(q, k, v, bm, n):
    t = q.shape[0]
    rows = n // GRID_W
    nqb = n // NA_QBLK
    assert nqb % NA_STEP_BLKS == 0 and nqb >= 5 and rows % 2 == 0
    npair = C_HEADS // 2
    step_rows = NA_STEP_BLKS * NA_QBLK
    kern = functools.partial(_natten_kernel, rows=rows, nqb=nqb)
    return pl.pallas_call(
        kern,
        grid=(npair, nqb // NA_STEP_BLKS),
        in_specs=[pl.BlockSpec((step_rows, LANES), lambda j, b: (b, j)),
                  pl.BlockSpec((t, LANES), lambda j, b: (0, j)),
                  pl.BlockSpec((t, LANES), lambda j, b: (0, j)),
                  pl.BlockSpec((5, 2, NA_QBLK, BAND_ROWS * GRID_W), lambda j, b: (0, j, 0, 0))],
        out_specs=pl.BlockSpec((step_rows, LANES), lambda j, b: (b, j)),
        out_shape=jax.ShapeDtypeStruct((n, D_MODEL), BF16),
        compiler_params=_cparams(2),
        name="natten",
    )(q, k, v, bm)


def _ctx_attn_kernel(q_ref, k_ref, v_ref, o_ref):
    q = q_ref[...]
    k = k_ref[...]
    v = v_ref[...]
    lane = lax.broadcasted_iota(jnp.int32, q.shape, 1)
    lo = lane < HEAD_DIM
    zero = jnp.zeros_like(q)
    outs = []
    for half in range(2):
        qh = jnp.where(lo, q, zero) if half == 0 else jnp.where(lo, zero, q)
        outs.append(_softmax_av([(_dot_nt(qh, k), v)]))
    o_ref[...] = jnp.where(lo, outs[0], outs[1]).astype(BF16)


def _ctx_attn(q, k, v):
    ctx_blk = q.shape[0] // CTX_LEN - 1
    spec = pl.BlockSpec((CTX_LEN, LANES), lambda j: (ctx_blk, j))
    return pl.pallas_call(
        _ctx_attn_kernel,
        grid=(C_HEADS // 2,),
        in_specs=[spec, spec, spec],
        out_specs=pl.BlockSpec((CTX_LEN, LANES), lambda j: (0, j)),
        out_shape=jax.ShapeDtypeStruct((CTX_LEN, D_MODEL), BF16),
        compiler_params=_cparams(1),
        name="ctx_attn",
    )(q, k, v)


def _natten_bias_tables(rpb, n):
    rows = n // GRID_W
    nb = rows // 2
    c = np.arange(GRID_W)
    sel = (np.arange(RPB_W)[:, None, None] == (c[None, None, :] - c[None, :, None] + NA_WIN_W - 1))
    toep = jnp.einsum("hdm,mck->hdck", rpb, jnp.asarray(sel, F32), precision=HIGHEST)
    cs = np.clip(c - NA_WIN_W // 2, 0, GRID_W - NA_WIN_W)
    kc = np.arange(GRID_W)
    col_ok = (kc[None, :] >= cs[:, None]) & (kc[None, :] < cs[:, None] + NA_WIN_W)
    pick = np.zeros((5, 2, BAND_ROWS, RPB_H), np.float32)
    for ti, b in enumerate((0, 1, 2, nb - 2, nb - 1)):
        band = int(np.clip(2 * b - NA_WIN_H // 2, 0, rows - BAND_ROWS))
        for qr in range(2):
            r = 2 * b + qr
            rs = int(np.clip(r - NA_WIN_H // 2, 0, rows - NA_WIN_H))
            for i in range(BAND_ROWS):
                kr = band + i
                if rs <= kr < rs + NA_WIN_H:
                    pick[ti, qr, i, kr - r + NA_WIN_H - 1] = 1.0
    row_ok = pick.sum(axis=3) > 0
    mask = np.where(row_ok[:, :, None, :, None] & col_ok[None, None, :, None, :], 0.0, NEG).astype(np.float32)
    bm = jnp.einsum("hdck,tqid->thqcik", toep, jnp.asarray(pick), precision=HIGHEST) + jnp.asarray(mask)[:, None]
    return bm.reshape(5, C_HEADS, NA_QBLK, BAND_ROWS * GRID_W)


def _post_attn_kernel(*refs, with_ctx, ctx_blk):
    if with_ctx:
        octx_ref, refs = refs[0], refs[1:]
    olat_ref, x_ref, mod_ref, wout_ref, lng_ref, lnb_ref, rwh_ref, rwl_ref, x1_ref, h2_ref, r_ref = refs
    o = olat_ref[...]
    if with_ctx:
        o = jnp.where(pl.program_id(0) == ctx_blk, octx_ref[...], o)
    mix = jnp.dot(o, wout_ref[...], preferred_element_type=F32)
    x1 = _layer_norm(DN_ALPHA * x_ref[...] + mod_ref[2:3, :] * mix, lng_ref[0:1, :], lnb_ref[0:1, :])
    x1_ref[...] = x1
    h2 = x1 * (1.0 + mod_ref[4:5, :]) + mod_ref[3:4, :]
    h2_ref[...] = h2
    h_hi = h2.astype(BF16)
    h_lo = (h2 - h_hi.astype(F32)).astype(BF16)
    logits = (jnp.dot(h_hi, rwh_ref[...], preferred_element_type=F32)
              + jnp.dot(h_hi, rwl_ref[...], preferred_element_type=F32)
              + jnp.dot(h_lo, rwh_ref[...], preferred_element_type=F32))
    lane = lax.broadcasted_iota(jnp.int32, logits.shape, 1).astype(F32)
    big = float(LANES)
    lg = jnp.where(lane < N_GROUPS, logits, NEG)
    mg = jnp.max(lg, axis=1, keepdims=True)
    gstar = jnp.min(jnp.where(lg == mg, lane, big), axis=1, keepdims=True)
    p_top = 1.0 / jnp.sum(jnp.exp(lg - mg), axis=1, keepdims=True)
    e_lo = N_GROUPS + EXPERTS_PER_GROUP * gstar
    le = jnp.where((lane >= e_lo) & (lane < e_lo + EXPERTS_PER_GROUP), logits, NEG)
    v1 = jnp.max(le, axis=1, keepdims=True)
    i1 = jnp.min(jnp.where(le == v1, lane, big), axis=1, keepdims=True)
    le2 = jnp.where(lane == i1, NEG, le)
    v2 = jnp.max(le2, axis=1, keepdims=True)
    i2 = jnp.min(jnp.where(le2 == v2, lane, big), axis=1, keepdims=True)
    e2 = jnp.exp(v2 - v1)
    w1 = p_top / (1.0 + e2)
    w2 = p_top * e2 / (1.0 + e2)
    r = jnp.where(lane == 0.0, i1 - N_GROUPS, 0.0)
    r = jnp.where(lane == 1.0, i2 - N_GROUPS, r)
    r = jnp.where(lane == 2.0, w1, r)
    r = jnp.where(lane == 3.0, w2, r)
    r_ref[...] = r


def _post_attn(o_ctx, o_lat, xt, modl, w_out, lng, lnb, rw, *, with_ctx):
    t = xt.shape[0]
    d = D_MODEL
    ctx_blk = (t - CTX_LEN) // ROW_BLK
    nblk = ctx_blk + 1 if with_ctx else ctx_blk
    row = pl.BlockSpec((ROW_BLK, d), lambda i: (i, 0))
    in_specs = [pl.BlockSpec((ROW_BLK, d), lambda i: (jnp.minimum(i, ctx_blk - 1), 0)),
                row, _mod_spec(ctx_blk),
                pl.BlockSpec((d, d), lambda i: (0, 0)),
                pl.BlockSpec((2, d), lambda i: (0, 0)),
                pl.BlockSpec((2, d), lambda i: (0, 0)),
                pl.BlockSpec((d, LANES), lambda i: (0, 0)),
                pl.BlockSpec((d, LANES), lambda i: (0, 0))]
    rw_hi = rw.astype(BF16)
    rw_lo = (rw - rw_hi.astype(F32)).astype(BF16)
    args = [o_lat, xt, modl, w_out, lng, lnb, rw_hi, rw_lo]
    if with_ctx:
        in_specs = [pl.BlockSpec((ROW_BLK, d), lambda i: (0, 0))] + in_specs
        args = [o_ctx] + args
    kern = functools.partial(_post_attn_kernel, with_ctx=with_ctx, ctx_blk=ctx_blk)
    return pl.pallas_call(
        kern,
        grid=(nblk,),
        in_specs=in_specs,
        out_specs=[row, row, pl.BlockSpec((ROW_BLK, LANES), lambda i: (i, 0))],
        out_shape=[jax.ShapeDtypeStruct((nblk * ROW_BLK, d), F32),
                   jax.ShapeDtypeStruct((nblk * ROW_BLK, d), F32),
                   jax.ShapeDtypeStruct((nblk * ROW_BLK, LANES), F32)],
        compiler_params=_cparams(1),
        name="post_attn",
    )(*args)


def _row_copies(n_rows, make):
    def start(r, c):
        for kk in range(2):
            make(r, kk).start(priority=kk)
        return c

    def wait(r, c):
        for kk in range(2):
            make(r, kk).wait()
        return c

    for r in range(n_rows):
        start(r, 0)
    lax.fori_loop(0, n_rows, wait, 0, unroll=ROW_COPY_UNROLL)


def _dispatch_kernel(dest_ref, h_ref, xs_in_ref, xs_ref, sem):
    del xs_in_ref

    def make(r, kk):
        dst = dest_ref[0, 0, 2 * r + kk]
        return pltpu.make_async_copy(h_ref.at[pl.ds(r, 1)], xs_ref.at[pl.ds(dst, 1)], sem)

    _row_copies(h_ref.shape[0], make)


def _dispatch(h2, dest3, xs0):
    nblk = dest3.shape[0]
    return pl.pallas_call(
        _dispatch_kernel,
        grid=(nblk,),
        in_specs=[pl.BlockSpec((1, 1, 2 * ROW_BLK), lambda i: (i, 0, 0), memory_space=pltpu.SMEM),
                  pl.BlockSpec((ROW_BLK, D_MODEL), lambda i: (i, 0)),
                  pl.BlockSpec(memory_space=pl.ANY)],
        out_specs=pl.BlockSpec(memory_space=pl.ANY),
        out_shape=jax.ShapeDtypeStruct(xs0.shape, xs0.dtype),
        scratch_shapes=[pltpu.SemaphoreType.DMA(())],
        input_output_aliases={2: 0},
        compiler_params=pltpu.CompilerParams(dimension_semantics=("arbitrary",),
                                             vmem_limit_bytes=VMEM_LIMIT, has_side_effects=True),
        name="moe_dispatch",
    )(dest3, h2, xs0)


def _experts_kernel(bexp_ref, nused_ref, xs_ref, w1_ref, w3_ref, w2_ref, y_ref, w1b, w3b, w2b):
    b = pl.program_id(0)

    @pl.when((b == 0) | (bexp_ref[b] != bexp_ref[jnp.maximum(b - 1, 0)]))
    def _():
        w1b[...] = w1_ref[...].astype(BF16)
        w3b[...] = w3_ref[...].astype(BF16)
        w2b[...] = w2_ref[...].astype(BF16)

    @pl.when(b < nused_ref[0])
    def _():
        x = xs_ref[...].astype(BF16)
        h1 = jnp.dot(x, w1b[...], preferred_element_type=F32)
        h3 = jnp.dot(x, w3b[...], preferred_element_type=F32)
        a = (_silu(h1) * h3).astype(BF16)
        y_ref[...] = jnp.dot(a, w2b[...], preferred_element_type=F32)

    @pl.when(b >= nused_ref[0])
    def _():
        y_ref[...] = jnp.zeros(y_ref.shape, F32)


def _experts(block_exp, nused, xs, w1, w3, w2, layer):
    n_rows = xs.shape[0]
    d = D_MODEL
    grid_spec = pltpu.PrefetchScalarGridSpec(
        num_scalar_prefetch=2,
        grid=(n_rows // MOE_BLK,),
        in_specs=[pl.BlockSpec((MOE_BLK, d), lambda b, be, nu: (jnp.minimum(b, nu[0] - 1), 0)),
                  pl.BlockSpec((None, None, d, D_EXPERT), lambda b, be, nu: (layer, be[b], 0, 0)),
                  pl.BlockSpec((None, None, d, D_EXPERT), lambda b, be, nu: (layer, be[b], 0, 0)),
                  pl.BlockSpec((None, None, D_EXPERT, d), lambda b, be, nu: (layer, be[b], 0, 0))],
        out_specs=pl.BlockSpec((MOE_BLK, d), lambda b, be, nu: (b, 0)),
        scratch_shapes=[pltpu.VMEM((d, D_EXPERT), BF16), pltpu.VMEM((d, D_EXPERT), BF16),
                        pltpu.VMEM((D_EXPERT, d), BF16)],
    )
    return pl.pallas_call(
        _experts_kernel,
        grid_spec=grid_spec,
        out_shape=jax.ShapeDtypeStruct((n_rows, d), F32),
        compiler_params=_cparams(1),
        name="moe_experts",
    )(block_exp, nused, xs, w1, w3, w2)


def _combine_kernel(dcur_ref, dnext_ref, x1_ref, r_ref, mod_ref, lng_ref, lnb_ref, y_ref, o_ref, ybuf, sem):
    i = pl.program_id(0)
    n_rows = x1_ref.shape[0]

    def gather(d_ref, dst_slot, r, kk):
        src = d_ref[0, 0, 2 * r + kk]
        return pltpu.make_async_copy(y_ref.at[pl.ds(src, 1)], ybuf.at[dst_slot, kk, pl.ds(r, 1)], sem.at[dst_slot])

    def start_rows(d_ref, dst_slot, r0, count):
        for rr in range(count):
            for kk in range(2):
                gather(d_ref, dst_slot, r0 + rr, kk).start(priority=kk)

    def wait_all(dst_slot):
        def body(r, c):
            for kk in range(2):
                gather(dcur_ref, dst_slot, r, kk).wait()
            return c
        lax.fori_loop(0, n_rows, body, 0, unroll=ROW_COPY_UNROLL)

    @pl.when(i == 0)
    def _():
        def body(g, c):
            start_rows(dcur_ref, 0, g * ROW_COPY_UNROLL, ROW_COPY_UNROLL)
            return c
        lax.fori_loop(0, n_rows // ROW_COPY_UNROLL, body, 0)

    g2 = mod_ref[5:6, :]
    lng = lng_ref[1:2, :]
    lnb = lnb_ref[1:2, :]

    def step(slot):
        other = 1 - slot

        start_rows(dnext_ref, other, 0, n_rows)
        wait_all(slot)
        ymix = ybuf[slot, 0] * r_ref[:, 2:3] + ybuf[slot, 1] * r_ref[:, 3:4]
        o_ref[...] = _layer_norm(DN_ALPHA * x1_ref[...] + g2 * ymix, lng, lnb)

        @pl.when(i == pl.num_programs(0) - 1)
        def _():
            wait_all(other)

    @pl.when(lax.rem(i, 2) == 0)
    def _():
        step(0)

    @pl.when(lax.rem(i, 2) == 1)
    def _():
        step(1)


def _combine(dest3, x1, r, modl, lng, lnb, y, *, ctx_blk):
    nblk = dest3.shape[0]
    d = D_MODEL
    return pl.pallas_call(
        _combine_kernel,
        grid=(nblk,),
        in_specs=[pl.BlockSpec((1, 1, 2 * ROW_BLK), lambda i: (i, 0, 0), memory_space=pltpu.SMEM),
                  pl.BlockSpec((1, 1, 2 * ROW_BLK), lambda i: (jnp.minimum(i + 1, nblk - 1), 0, 0),
                               memory_space=pltpu.SMEM),
                  pl.BlockSpec((ROW_BLK, d), lambda i: (i, 0)),
                  pl.BlockSpec((ROW_BLK, LANES), lambda i: (i, 0)),
                  _mod_spec(ctx_blk),
                  pl.BlockSpec((2, d), lambda i: (0, 0)),
                  pl.BlockSpec((2, d), lambda i: (0, 0)),
                  pl.BlockSpec(memory_space=pl.ANY)],
        out_specs=pl.BlockSpec((ROW_BLK, d), lambda i: (i, 0)),
        out_shape=jax.ShapeDtypeStruct((nblk * ROW_BLK, d), F32),
        scratch_shapes=[pltpu.VMEM((2, 2, ROW_BLK, d), F32), pltpu.SemaphoreType.DMA((2,))],
        compiler_params=_cparams(1),
        name="moe_combine",
    )(dest3, dest3, x1, r, modl, lng, lnb, y)


def _dispatch_plan(r):
    ntok = r.shape[0]
    ef = r[:, 0:2].astype(jnp.int32).reshape(-1)
    na = 2 * ntok
    sub = 256
    oh = (ef[:, None] == jnp.arange(N_EXPERTS, dtype=jnp.int32)[None, :]).astype(F32)
    ohb = oh.reshape(na // sub, sub, N_EXPERTS)
    tril = jnp.tril(jnp.ones((sub, sub), F32))
    within = jnp.einsum("ij,bjk->bik", tril, ohb)
    tot = within[:, -1, :]
    before = jnp.cumsum(tot, axis=0) - tot
    rank = jnp.sum((within + before[:, None, :]) * ohb, axis=2).reshape(na) - 1.0
    counts = jnp.sum(tot, axis=0).astype(jnp.int32)
    padded = (counts + MOE_BLK - 1) // MOE_BLK * MOE_BLK
    pad_ends = jnp.cumsum(padded)
    pad_starts = pad_ends - padded
    dest = jnp.sum(oh * pad_starts.astype(F32)[None, :], axis=1) + rank
    dest = dest.astype(jnp.int32)
    n_blocks = na // MOE_BLK + N_EXPERTS
    blk_start = jnp.arange(n_blocks, dtype=jnp.int32) * MOE_BLK
    block_exp = jnp.sum((pad_ends[None, :] <= blk_start[:, None]).astype(jnp.int32), axis=1)
    block_exp = jnp.minimum(block_exp, N_EXPERTS - 1)
    nused = (pad_ends[-1:] // MOE_BLK).astype(jnp.int32)
    return dest.reshape(ntok // ROW_BLK, 1, 2 * ROW_BLK), block_exp, nused, n_blocks


def _moe(h2, x1, r, modl, lng, lnb, w1, w3, w2, *, layer, ctx_blk):
    dest3, block_exp, nused, n_blocks = _dispatch_plan(r)
    xs0 = jnp.zeros((n_blocks * MOE_BLK, D_MODEL), F32)
    xs = _dispatch(h2, dest3, xs0)
    y = _experts(block_exp, nused, xs, w1, w3, w2, layer)
    return _combine(dest3, x1, r, modl, lng, lnb, y, ctx_blk=ctx_blk)


def _rope_tables(n):
    tkn = jnp.arange(n, dtype=jnp.int32)
    row = (tkn // GRID_W).astype(F32)
    col = (tkn % GRID_W).astype(F32)
    half = HEAD_DIM // 2
    inv = ROPE_THETA ** (-jnp.arange(0, half, 2, dtype=F32) / half)
    ang_r = row[:, None] * inv[None, :]
    ang_c = col[:, None] * inv[None, :]
    ang = jnp.concatenate([ang_r, ang_r, ang_c, ang_c] * 2, axis=-1)
    cos = jnp.concatenate([jnp.cos(ang), jnp.ones((CTX_LEN, LANES), F32)], axis=0)
    sin = jnp.concatenate([jnp.sin(ang), jnp.zeros((CTX_LEN, LANES), F32)], axis=0)
    return cos, sin


def _kv_block(t):
    for cand in (1280, 1024, 768, 512, 256):
        if t % cand == 0:
            return cand
    raise ValueError(f"unsupported token count {t}")


def kernel(x, c, ctx, c_ctx, w_mod, b_mod, ln_g, ln_b, w_in_even, w_out_even, diff_lam, diff_subln_g,
           gqa_qk_g, w_in_odd, w_out_odd, na_rpb, router_g, router_e, w1, w3, w2):
    b, n, d = x.shape
    assert b == 1 and d == D_MODEL and ctx.shape == (1, CTX_LEN, D_MODEL)
    assert n % ROW_BLK == 0 and n // GRID_W >= BAND_ROWS and CTX_LEN == ROW_BLK
    t = n + CTX_LEN
    ctx_blk = n // ROW_BLK
    xt = jnp.concatenate([x[0], ctx[0]], axis=0)
    cvec = jnp.zeros((8, d), F32).at[0].set(c[0]).at[1].set(c_ctx)
    mod = _modulation(cvec, w_mod, b_mod)[:, 0:2, :].reshape(DEPTH, 2, 6, d)
    cos, sin = _rope_tables(n)
    hm = jnp.asarray(np.kron(np.eye(2), np.full((HEAD_DIM, HEAD_DIM), 1.0 / HEAD_DIM)), F32)
    rw = jnp.zeros((DEPTH, d, LANES), F32)
    rw = rw.at[:, :, 0:N_GROUPS].set(router_g).at[:, :, N_GROUPS:N_GROUPS + N_EXPERTS].set(router_e)
    bk = _kv_block(t)
    bq = ATTN_BQ if n % ATTN_BQ == 0 else ROW_BLK
    out = None
    for l in range(DEPTH):
        last = l == DEPTH - 1
        i = l // 2
        modl = mod[l]
        if l % 2 == 0:
            lam_init = 0.8 - 0.6 * math.exp(-0.3 * l)
            q, k, v, qn, kmx = _proj_even(xt, modl, w_in_even[i].astype(BF16), cos, sin,
                                          jnp.tile(gqa_qk_g[i], (1, 2)), hm)
            subg = diff_subln_g[i].reshape(LANES, 1)
            o_lat = _attn_even(q, k, v, qn, kmx, diff_lam[i], subg, q_row0=0, n_q=n, k_row0=0, n_k=t,
                               bq=bq, bk=bk, lam_init=lam_init)
            o_ctx = None if last else _attn_even(q, k, v, qn, kmx, diff_lam[i], subg, q_row0=n, n_q=CTX_LEN,
                                                 k_row0=n, n_k=CTX_LEN, bq=CTX_LEN, bk=CTX_LEN,
                                                 lam_init=lam_init)
            w_out = w_out_even[i]
        else:
            q, k, v = _proj_odd(xt, modl, w_in_odd[i].astype(BF16))
            bm = _natten_bias_tables(na_rpb[i], n)
            o_lat = _natten(q, k, v, bm, n)
            o_ctx = None if last else _ctx_attn(q, k, v)
            w_out = w_out_odd[i]
        x1, h2, r = _post_attn(o_ctx, o_lat, xt, modl, w_out.astype(BF16), ln_g[l], ln_b[l], rw[l],
                               with_ctx=not last)
        xt = _moe(h2, x1, r, modl, ln_g[l], ln_b[l], w1, w3, w2, layer=l, ctx_blk=ctx_blk)
        out = xt
    return out.reshape(1, n, d)
```

```python
import functools
import math

import numpy as np
import jax
import jax.numpy as jnp
from jax import lax
from jax.experimental import pallas as pl
from jax.experimental.pallas import tpu as pltpu

D_MODEL = 1024
DEPTH = 4
GRID_W = 64
CTX_LEN = 256
HEAD_DIM = 64
A_HEADS = 4
B_HEADS = 8
B_KV_HEADS = 2
C_HEADS = D_MODEL // HEAD_DIM
NA_WIN_H = 8
NA_WIN_W = 16
RPB_H = 2 * NA_WIN_H - 1
RPB_W = 2 * NA_WIN_W - 1
N_GROUPS = 4
EXPERTS_PER_GROUP = 8
N_EXPERTS = N_GROUPS * EXPERTS_PER_GROUP
D_EXPERT = 512
ROPE_THETA = 10000.0
DN_ALPHA = (2.0 * DEPTH) ** 0.25
EVEN_IN = 2304
LN_EPS = 1e-6

LANES = 128
ROW_BLK = 256
MOE_BLK = 512
ROW_COPY_UNROLL = 8
N_MAPS = 16
N_KV_CHUNKS = 5
ATTN_BQ = 512
LOG2E = math.log2(math.e)
SAFE_LOG2_GAP = 80.0
NORM_SLACK = 1.001
BAND_ROWS = 10
NA_QBLK = 2 * GRID_W
NA_STEP_BLKS = 8
NEG = -1e30
VMEM_LIMIT = 48 * 1024 * 1024

F32 = jnp.float32
BF16 = jnp.bfloat16
HIGHEST = lax.Precision.HIGHEST


def _cparams(n_axes):
    return pltpu.CompilerParams(dimension_semantics=("arbitrary",) * n_axes,
                                vmem_limit_bytes=VMEM_LIMIT)


def _dot_nt(a, b):
    return lax.dot_general(a, b, (((1,), (1,)), ((), ())), preferred_element_type=F32)


def _layer_norm(v, g, b):
    mu = jnp.mean(v, axis=-1, keepdims=True)
    d = v - mu
    var = jnp.mean(d * d, axis=-1, keepdims=True)
    return d * lax.rsqrt(var + LN_EPS) * g + b


def _silu(v):
    return v / (1.0 + jnp.exp(-v))


def _mod_kernel(c_ref, w_ref, b_ref, o_ref):
    s = _silu(c_ref[...])
    o_ref[...] = jnp.dot(s, w_ref[...], preferred_element_type=F32, precision=HIGHEST) + b_ref[...]


def _modulation(cvec, w_mod, b_mod):
    d = D_MODEL
    return pl.pallas_call(
        _mod_kernel,
        grid=(DEPTH, 6),
        in_specs=[pl.BlockSpec((8, d), lambda l, j: (0, 0)),
                  pl.BlockSpec((None, d, d), lambda l, j: (l, 0, j)),
                  pl.BlockSpec((None, 1, d), lambda l, j: (l, 0, j))],
        out_specs=pl.BlockSpec((None, 8, d), lambda l, j: (l, 0, j)),
        out_shape=jax.ShapeDtypeStruct((DEPTH, 8, 6 * d), F32),
        compiler_params=_cparams(2),
        name="modulation",
    )(cvec, w_mod, b_mod.reshape(DEPTH, 1, 6 * d))


def _mod_spec(ctx_blk):
    return pl.BlockSpec((None, 6, D_MODEL), lambda i: (jnp.where(i == ctx_blk, 1, 0), 0, 0))


def _rope(z, cos, sin, first16):
    rot = jnp.where(first16, -pltpu.roll(z, LANES - 16, 1), pltpu.roll(z, 16, 1))
    return z * cos + rot * sin


def _proj_even_kernel(x_ref, mod_ref, w_ref, cos_ref, sin_ref, g_ref, hm_ref, qt_ref, k_ref, vt_ref, qn_ref, kmx_ref):
    x = x_ref[...]
    h = (x * (1.0 + mod_ref[1:2, :]) + mod_ref[0:1, :]).astype(BF16)
    y = jnp.dot(h, w_ref[...], preferred_element_type=F32)
    cos = cos_ref[...]
    sin = sin_ref[...]
    lane = lax.broadcasted_iota(jnp.int32, cos.shape, 1)
    first16 = (lane % 32) < 16
    lo = lane < HEAD_DIM
    hm = hm_ref[...]
    scale = HEAD_DIM ** -0.5 * LOG2E

    def chunk(c):
        return y[:, c * LANES:(c + 1) * LANES]

    def qk_norm(z, g):
        ms = jnp.dot(z * z, hm, preferred_element_type=F32, precision=HIGHEST)
        return z * lax.rsqrt(ms + LN_EPS) * g

    def put_t(ref, idx, val):
        ref[idx] = val.T.astype(BF16)

    def put_q(idx, val):
        vb = val.T.astype(BF16)
        qt_ref[idx] = vb
        vf = vb.astype(F32)
        qn_ref[idx:idx + 1, :] = jnp.sqrt(jnp.sum(vf * vf, axis=0, keepdims=True))

    kmax = jnp.zeros(kmx_ref.shape, F32)
    klane = lax.broadcasted_iota(jnp.int32, kmx_ref.shape, 1)

    def put_k(idx, val, kmax):
        kb = val.astype(BF16)
        k_ref[idx] = kb
        kf = kb.astype(F32)
        n2 = jnp.max(jnp.sum(kf * kf, axis=1, keepdims=True), axis=0, keepdims=True)
        return jnp.where(klane == idx, n2, kmax)

    for hd in range(A_HEADS):
        q = _rope(chunk(hd), cos, sin, first16) * scale
        put_q(hd, jnp.where(lo, q, 0.0))
        put_q(A_HEADS + hd, jnp.where(lo, 0.0, q))
        kmax = put_k(hd, _rope(chunk(4 + hd), cos, sin, first16), kmax)
        put_t(vt_ref, hd, chunk(8 + hd))
    gq = g_ref[0:1, :]
    gk = g_ref[1:2, :]
    for pr in range(B_HEADS // 2):
        q = _rope(qk_norm(chunk(12 + pr), gq), cos, sin, first16) * scale
        qs = pltpu.roll(q, HEAD_DIM, 1)
        if pr < 2:
            put_q(8 + 2 * pr, jnp.where(lo, q, 0.0))
            put_q(8 + 2 * pr + 1, jnp.where(lo, qs, 0.0))
        else:
            put_q(8 + 2 * pr, jnp.where(lo, 0.0, qs))
            put_q(8 + 2 * pr + 1, jnp.where(lo, 0.0, q))
    kmax = put_k(4, _rope(qk_norm(chunk(16), gk), cos, sin, first16), kmax)
    put_t(vt_ref, 4, chunk(17))
    kmx_ref[...] = kmax


def _proj_even(xt, modl, w_in, cos, sin, g2, hm):
    t = xt.shape[0]
    nb = t // ROW_BLK
    return pl.pallas_call(
        _proj_even_kernel,
        grid=(nb,),
        in_specs=[pl.BlockSpec((ROW_BLK, D_MODEL), lambda i: (i, 0)),
                  _mod_spec(nb - 1),
                  pl.BlockSpec((D_MODEL, EVEN_IN), lambda i: (0, 0)),
                  pl.BlockSpec((ROW_BLK, LANES), lambda i: (i, 0)),
                  pl.BlockSpec((ROW_BLK, LANES), lambda i: (i, 0)),
                  pl.BlockSpec((2, LANES), lambda i: (0, 0)),
                  pl.BlockSpec((LANES, LANES), lambda i: (0, 0))],
        out_specs=[pl.BlockSpec((N_MAPS, LANES, ROW_BLK), lambda i: (0, 0, i)),
                   pl.BlockSpec((N_KV_CHUNKS, ROW_BLK, LANES), lambda i: (0, i, 0)),
                   pl.BlockSpec((N_KV_CHUNKS, LANES, ROW_BLK), lambda i: (0, 0, i)),
                   pl.BlockSpec((N_MAPS, ROW_BLK), lambda i: (0, i)),
                   pl.BlockSpec((None, 8, LANES), lambda i: (i, 0, 0))],
        out_shape=[jax.ShapeDtypeStruct((N_MAPS, LANES, t), BF16),
                   jax.ShapeDtypeStruct((N_KV_CHUNKS, t, LANES), BF16),
                   jax.ShapeDtypeStruct((N_KV_CHUNKS, LANES, t), BF16),
                   jax.ShapeDtypeStruct((N_MAPS, t), F32),
                   jax.ShapeDtypeStruct((nb, 8, LANES), F32)],
        compiler_params=_cparams(1),
        name="proj_even",
    )(xt, modl, w_in, cos, sin, g2, hm)


def _kv_chunk(mm):
    return mm % A_HEADS if mm < 2 * A_HEADS else A_HEADS


def _attn_even_kernel(kmax_ref, lam_ref, subg_ref, qn_ref, qt_ref, k_ref, vt_ref, o_ref, m_sc, l_sc, acc_sc, mg_sc,
                      *, nkv, lam_init):
    j = pl.program_id(1)

    @pl.when(j == 0)
    def _():
        m_sc[...] = jnp.full(m_sc.shape, NEG, F32)
        mg_sc[...] = jnp.full(mg_sc.shape, NEG, F32)
        l_sc[...] = jnp.zeros(l_sc.shape, F32)
        acc_sc[...] = jnp.zeros(acc_sc.shape, F32)

    sub = lax.broadcasted_iota(jnp.int32, (N_MAPS, 1), 0)
    chunk_of_map = jnp.where(sub >= 2 * A_HEADS, A_HEADS, sub & (A_HEADS - 1))
    kvec = jnp.zeros((N_MAPS, 1), F32)
    for c in range(N_KV_CHUNKS):
        kvec = jnp.where(chunk_of_map == c, kmax_ref[j, c] * NORM_SLACK, kvec)
    keep_ref = jnp.max(qn_ref[...] * kvec - mg_sc[...]) < SAFE_LOG2_GAP

    def scores(mm):
        return jnp.dot(k_ref[_kv_chunk(mm)], qt_ref[mm], preferred_element_type=F32)

    def accumulate_keep(mm, s):
        p = jnp.exp2(s - m_sc[mm])
        l_sc[mm] = l_sc[mm] + jnp.sum(p, axis=0, keepdims=True)
        acc_sc[mm] = acc_sc[mm] + jnp.dot(vt_ref[_kv_chunk(mm)], p.astype(BF16), preferred_element_type=F32)

    def accumulate_move(mm, s):
        m_prev = m_sc[mm]
        m_new = jnp.maximum(m_prev, jnp.max(s, axis=0, keepdims=True))
        alpha = jnp.exp2(m_prev - m_new)
        p = jnp.exp2(s - m_new)
        l_sc[mm] = alpha * l_sc[mm] + jnp.sum(p, axis=0, keepdims=True)
        pv = jnp.dot(vt_ref[_kv_chunk(mm)], p.astype(BF16), preferred_element_type=F32)
        acc_sc[mm] = alpha * acc_sc[mm] + pv
        m_sc[mm] = m_new
        mg_sc[mm:mm + 1, :] = m_new

    def all_maps(accumulate):
        s_next = scores(0)
        for mm in range(N_MAPS):
            s_cur = s_next
            if mm + 1 < N_MAPS:
                s_next = scores(mm + 1)
            accumulate(mm, s_cur)

    @pl.when(keep_ref)
    def _():
        all_maps(accumulate_keep)

    @pl.when(jnp.logical_not(keep_ref))
    def _():
        all_maps(accumulate_move)

    @pl.when(j == nkv - 1)
    def _():
        lamv = lam_ref[...]
        e1 = jnp.exp(jnp.sum(lamv[0:1, :] * lamv[1:2, :], axis=1, keepdims=True))
        e2 = jnp.exp(jnp.sum(lamv[2:3, :] * lamv[3:4, :], axis=1, keepdims=True))
        lam = e1 - e2 + lam_init
        subg = subg_ref[...]
        for hd in range(A_HEADS):
            o = acc_sc[hd] / l_sc[hd] - lam * (acc_sc[A_HEADS + hd] / l_sc[A_HEADS + hd])
            ms = jnp.mean(o * o, axis=0, keepdims=True)
            o = o * lax.rsqrt(ms + LN_EPS) * subg * (1.0 - lam_init)
            o_ref[:, hd * LANES:(hd + 1) * LANES] = o.T.astype(BF16)
        for pr in range(B_HEADS // 2):
            oa = acc_sc[8 + 2 * pr] / l_sc[8 + 2 * pr]
            ob = acc_sc[8 + 2 * pr + 1] / l_sc[8 + 2 * pr + 1]
            r0 = 0 if pr < 2 else HEAD_DIM
            o = jnp.concatenate([oa[r0:r0 + HEAD_DIM, :], ob[r0:r0 + HEAD_DIM, :]], axis=0)
            o_ref[:, (A_HEADS + pr) * LANES:(A_HEADS + pr + 1) * LANES] = o.T.astype(BF16)


def _attn_even(qt, k, vt, qn, kmx, lamf, subg, *, q_row0, n_q, k_row0, n_k, bq, bk, lam_init):
    assert q_row0 % bq == 0 and n_q % bq == 0 and k_row0 % bk == 0 and n_k % bk == 0 and bk % ROW_BLK == 0
    nq = n_q // bq
    nkv = n_k // bk
    qoff = q_row0 // bq
    koff = k_row0 // bk
    kmax = kmx[k_row0 // ROW_BLK:(k_row0 + n_k) // ROW_BLK, 0, 0:8]
    kmax = jnp.sqrt(jnp.max(kmax.reshape(nkv, bk // ROW_BLK, 8), axis=1))
    kern = functools.partial(_attn_even_kernel, nkv=nkv, lam_init=lam_init)
    return pl.pallas_call(
        kern,
        grid=(nq, nkv),
        in_specs=[pl.BlockSpec(memory_space=pltpu.SMEM),
                  pl.BlockSpec((4, HEAD_DIM), lambda i, j: (0, 0)),
                  pl.BlockSpec((LANES, 1), lambda i, j: (0, 0)),
                  pl.BlockSpec((N_MAPS, bq), lambda i, j: (0, i + qoff)),
                  pl.BlockSpec((N_MAPS, LANES, bq), lambda i, j: (0, 0, i + qoff)),
                  pl.BlockSpec((N_KV_CHUNKS, bk, LANES), lambda i, j: (0, j + koff, 0)),
                  pl.BlockSpec((N_KV_CHUNKS, LANES, bk), lambda i, j: (0, 0, j + koff))],
        out_specs=pl.BlockSpec((bq, D_MODEL), lambda i, j: (i, 0)),
        out_shape=jax.ShapeDtypeStruct((n_q, D_MODEL), BF16),
        scratch_shapes=[pltpu.VMEM((N_MAPS, 1, bq), F32), pltpu.VMEM((N_MAPS, 1, bq), F32),
                        pltpu.VMEM((N_MAPS, LANES, bq), F32), pltpu.VMEM((N_MAPS, bq), F32)],
        compiler_params=_cparams(2),
        name="attn_even",
    )(kmax, lamf, subg, qn, qt, k, vt)


def _proj_odd_kernel(x_ref, mod_ref, w_ref, q_ref, k_ref, v_ref):
    x = x_ref[...]
    h = (x * (1.0 + mod_ref[1:2, :]) + mod_ref[0:1, :]).astype(BF16)
    d = D_MODEL
    q_ref[...] = (jnp.dot(h, w_ref[:, 0:d], preferred_element_type=F32) * (HEAD_DIM ** -0.5)).astype(BF16)
    k_ref[...] = jnp.dot(h, w_ref[:, d:2 * d], preferred_element_type=F32).astype(BF16)
    v_ref[...] = jnp.dot(h, w_ref[:, 2 * d:3 * d], preferred_element_type=F32).astype(BF16)


def _proj_odd(xt, modl, w_in):
    t = xt.shape[0]
    d = D_MODEL
    spec = pl.BlockSpec((ROW_BLK, d), lambda i: (i, 0))
    return pl.pallas_call(
        _proj_odd_kernel,
        grid=(t // ROW_BLK,),
        in_specs=[spec, _mod_spec(t // ROW_BLK - 1), pl.BlockSpec((d, 3 * d), lambda i: (0, 0))],
        out_specs=[spec, spec, spec],
        out_shape=[jax.ShapeDtypeStruct((t, d), BF16)] * 3,
        compiler_params=_cparams(1),
        name="proj_odd",
    )(xt, modl, w_in)


def _softmax_av(parts):
    m = None
    for s, _ in parts:
        sm = jnp.max(s, axis=1, keepdims=True)
        m = sm if m is None else jnp.maximum(m, sm)
    l = None
    o = None
    for s, v in parts:
        p = jnp.exp(s - m)
        ps = jnp.sum(p, axis=1, keepdims=True)
        pv = jnp.dot(p.astype(BF16), v, preferred_element_type=F32)
        l = ps if l is None else l + ps
        o = pv if o is None else o + pv
    return o / l


def _natten_kernel(q_ref, k_ref, v_ref, bm_ref, o_ref, *, rows, nqb):
    n = rows * GRID_W
    nband = BAND_ROWS * GRID_W
    kc = k_ref[n:n + CTX_LEN, :]
    vc = v_ref[n:n + CTX_LEN, :]
    lane = lax.broadcasted_iota(jnp.int32, (NA_QBLK, LANES), 1)
    lo = lane < HEAD_DIM
    def scores(sub):
        b = pl.program_id(1) * NA_STEP_BLKS + sub
        band = jnp.clip(2 * b - NA_WIN_H // 2, 0, rows - BAND_ROWS)
        typ = jnp.where(b < 2, b, jnp.where(b >= nqb - 2, b - (nqb - 5), 2))
        start = pl.multiple_of(band * GRID_W, GRID_W)
        kb = k_ref[pl.ds(start, nband), :]
        vb = v_ref[pl.ds(start, nband), :]
        q = q_ref[sub * NA_QBLK:(sub + 1) * NA_QBLK, :]
        zero = jnp.zeros_like(q)
        qs = jnp.concatenate([jnp.where(lo, q, zero), jnp.where(lo, zero, q)], axis=0)
        return _dot_nt(qs, kb), _dot_nt(qs, kc), typ, vb

    nxt = scores(0)
    for sub in range(NA_STEP_BLKS):
        s_nb, s_cx, typ, vb = nxt
        if sub + 1 < NA_STEP_BLKS:
            nxt = scores(sub + 1)
        bias = bm_ref[typ].reshape(2 * NA_QBLK, nband)
        o = _softmax_av([(s_nb + bias, vb), (s_cx, vc)])
        o_ref[sub * NA_QBLK:(sub + 1) * NA_QBLK, :] = jnp.where(lo, o[:NA_QBLK], o[NA_QBLK:]).astype(BF16)


def _natten(q, k, v, bm, n):
    t = q.shape[0]
    rows = n // GRID_W
    nqb = n // NA_QBLK
    assert nqb % NA_STEP_BLKS == 0 and nqb >= 5 and rows % 2 == 0
    npair = C_HEADS // 2
    step_rows = NA_STEP_BLKS * NA_QBLK
    kern = functools.partial(_natten_kernel, rows=rows, nqb=nqb)
    return pl.pallas_call(
        kern,
        grid=(npair, nqb // NA_STEP_BLKS),
        in_specs=[pl.BlockSpec((step_rows, LANES), lambda j, b: (b, j)),
                  pl.BlockSpec((t, LANES), lambda j, b: (0, j)),
                  pl.BlockSpec((t, LANES), lambda j, b: (0, j)),
                  pl.BlockSpec((5, 2, NA_QBLK, BAND_ROWS * GRID_W), lambda j, b: (0, j, 0, 0))],
        out_specs=pl.BlockSpec((step_rows, LANES), lambda j, b: (b, j)),
        out_shape=jax.ShapeDtypeStruct((n, D_MODEL), BF16),
        compiler_params=_cparams(2),
        name="natten",
    )(q, k, v, bm)


def _ctx_attn_kernel(q_ref, k_ref, v_ref, o_ref):
    q = q_ref[...]
    k = k_ref[...]
    v = v_ref[...]
    lane = lax.broadcasted_iota(jnp.int32, q.shape, 1)
    lo = lane < HEAD_DIM
    zero = jnp.zeros_like(q)
    outs = []
    for half in range(2):
        qh = jnp.where(lo, q, zero) if half == 0 else jnp.where(lo, zero, q)
        outs.append(_softmax_av([(_dot_nt(qh, k), v)]))
    o_ref[...] = jnp.where(lo, outs[0], outs[1]).astype(BF16)


def _ctx_attn(q, k, v):
    ctx_blk = q.shape[0] // CTX_LEN - 1
    spec = pl.BlockSpec((CTX_LEN, LANES), lambda j: (ctx_blk, j))
    return pl.pallas_call(
        _ctx_attn_kernel,
        grid=(C_HEADS // 2,),
        in_specs=[spec, spec, spec],
        out_specs=pl.BlockSpec((CTX_LEN, LANES), lambda j: (0, j)),
        out_shape=jax.ShapeDtypeStruct((CTX_LEN, D_MODEL), BF16),
        compiler_params=_cparams(1),
        name="ctx_attn",
    )(q, k, v)


def _natten_bias_tables(rpb, n):
    rows = n // GRID_W
    nb = rows // 2
    c = np.arange(GRID_W)
    sel = (np.arange(RPB_W)[:, None, None] == (c[None, None, :] - c[None, :, None] + NA_WIN_W - 1))
    toep = jnp.einsum("hdm,mck->hdck", rpb, jnp.asarray(sel, F32), precision=HIGHEST)
    cs = np.clip(c - NA_WIN_W // 2, 0, GRID_W - NA_WIN_W)
    kc = np.arange(GRID_W)
    col_ok = (kc[None, :] >= cs[:, None]) & (kc[None, :] < cs[:, None] + NA_WIN_W)
    colmask = jnp.asarray(np.where(col_ok, 0.0, NEG), F32)
    negblk = jnp.full((C_HEADS, GRID_W, GRID_W), NEG, F32)
    tables = []
    for b in (0, 1, 2, nb - 2, nb - 1):
        band = int(np.clip(2 * b - NA_WIN_H // 2, 0, rows - BAND_ROWS))
        qrows = []
        for qr in range(2):
            r = 2 * b + qr
            rs = int(np.clip(r - NA_WIN_H // 2, 0, rows - NA_WIN_H))
            blks = []
            for i in range(BAND_ROWS):
                kr = band + i
                if rs <= kr < rs + NA_WIN_H:
                    blks.append(toep[:, kr - r + NA_WIN_H - 1] + colmask[None])
                else:
                    blks.append(negblk)
            qrows.append(jnp.concatenate(blks, axis=2))
        tables.append(jnp.concatenate(qrows, axis=1))
    return jnp.stack(tables, axis=0)


def _post_attn_kernel(*refs, with_ctx, ctx_blk):
    if with_ctx:
        octx_ref, refs = refs[0], refs[1:]
    olat_ref, x_ref, mod_ref, wout_ref, lng_ref, lnb_ref, rwh_ref, rwl_ref, x1_ref, h2_ref, r_ref = refs
    o = olat_ref[...]
    if with_ctx:
        o = jnp.where(pl.program_id(0) == ctx_blk, octx_ref[...], o)
    mix = jnp.dot(o, wout_ref[...], preferred_element_type=F32)
    x1 = _layer_norm(DN_ALPHA * x_ref[...] + mod_ref[2:3, :] * mix, lng_ref[0:1, :], lnb_ref[0:1, :])
    x1_ref[...] = x1
    h2 = x1 * (1.0 + mod_ref[4:5, :]) + mod_ref[3:4, :]
    h2_ref[...] = h2
    h_hi = h2.astype(BF16)
    h_lo = (h2 - h_hi.astype(F32)).astype(BF16)
    logits = (jnp.dot(h_hi, rwh_ref[...], preferred_element_type=F32)
              + jnp.dot(h_hi, rwl_ref[...], preferred_element_type=F32)
              + jnp.dot(h_lo, rwh_ref[...], preferred_element_type=F32))
    lane = lax.broadcasted_iota(jnp.int32, logits.shape, 1).astype(F32)
    big = float(LANES)
    lg = jnp.where(lane < N_GROUPS, logits, NEG)
    mg = jnp.max(lg, axis=1, keepdims=True)
    gstar = jnp.min(jnp.where(lg == mg, lane, big), axis=1, keepdims=True)
    p_top = 1.0 / jnp.sum(jnp.exp(lg - mg), axis=1, keepdims=True)
    e_lo = N_GROUPS + EXPERTS_PER_GROUP * gstar
    le = jnp.where((lane >= e_lo) & (lane < e_lo + EXPERTS_PER_GROUP), logits, NEG)
    v1 = jnp.max(le, axis=1, keepdims=True)
    i1 = jnp.min(jnp.where(le == v1, lane, big), axis=1, keepdims=True)
    le2 = jnp.where(lane == i1, NEG, le)
    v2 = jnp.max(le2, axis=1, keepdims=True)
    i2 = jnp.min(jnp.where(le2 == v2, lane, big), axis=1, keepdims=True)
    e2 = jnp.exp(v2 - v1)
    w1 = p_top / (1.0 + e2)
    w2 = p_top * e2 / (1.0 + e2)
    r = jnp.where(lane == 0.0, i1 - N_GROUPS, 0.0)
    r = jnp.where(lane == 1.0, i2 - N_GROUPS, r)
    r = jnp.where(lane == 2.0, w1, r)
    r = jnp.where(lane == 3.0, w2, r)
    r_ref[...] = r


def _post_attn(o_ctx, o_lat, xt, modl, w_out, lng, lnb, rw, *, with_ctx):
    t = xt.shape[0]
    d = D_MODEL
    ctx_blk = (t - CTX_LEN) // ROW_BLK
    nblk = ctx_blk + 1 if with_ctx else ctx_blk
    row = pl.BlockSpec((ROW_BLK, d), lambda i: (i, 0))
    in_specs = [pl.BlockSpec((ROW_BLK, d), lambda i: (jnp.minimum(i, ctx_blk - 1), 0)),
                row, _mod_spec(ctx_blk),
                pl.BlockSpec((d, d), lambda i: (0, 0)),
                pl.BlockSpec((2, d), lambda i: (0, 0)),
                pl.BlockSpec((2, d), lambda i: (0, 0)),
                pl.BlockSpec((d, LANES), lambda i: (0, 0)),
                pl.BlockSpec((d, LANES), lambda i: (0, 0))]
    rw_hi = rw.astype(BF16)
    rw_lo = (rw - rw_hi.astype(F32)).astype(BF16)
    args = [o_lat, xt, modl, w_out, lng, lnb, rw_hi, rw_lo]
    if with_ctx:
        in_specs = [pl.BlockSpec((ROW_BLK, d), lambda i: (0, 0))] + in_specs
        args = [o_ctx] + args
    kern = functools.partial(_post_attn_kernel, with_ctx=with_ctx, ctx_blk=ctx_blk)
    return pl.pallas_call(
        kern,
        grid=(nblk,),
        in_specs=in_specs,
        out_specs=[row, row, pl.BlockSpec((ROW_BLK, LANES), lambda i: (i, 0))],
        out_shape=[jax.ShapeDtypeStruct((nblk * ROW_BLK, d), F32),
                   jax.ShapeDtypeStruct((nblk * ROW_BLK, d), F32),
                   jax.ShapeDtypeStruct((nblk * ROW_BLK, LANES), F32)],
        compiler_params=_cparams(1),
        name="post_attn",
    )(*args)


def _row_copies(n_rows, make):
    def start(r, c):
        for kk in range(2):
            make(r, kk).start(priority=kk)
        return c

    def wait(r, c):
        for kk in range(2):
            make(r, kk).wait()
        return c

    for r in range(n_rows):
        start(r, 0)
    lax.fori_loop(0, n_rows, wait, 0, unroll=ROW_COPY_UNROLL)


def _dispatch_kernel(dest_ref, h_ref, xs_in_ref, xs_ref, sem):
    del xs_in_ref

    def make(r, kk):
        dst = dest_ref[0, 0, 2 * r + kk]
        return pltpu.make_async_copy(h_ref.at[pl.ds(r, 1)], xs_ref.at[pl.ds(dst, 1)], sem)

    _row_copies(h_ref.shape[0], make)


def _dispatch(h2, dest3, xs0):
    nblk = dest3.shape[0]
    return pl.pallas_call(
        _dispatch_kernel,
        grid=(nblk,),
        in_specs=[pl.BlockSpec((1, 1, 2 * ROW_BLK), lambda i: (i, 0, 0), memory_space=pltpu.SMEM),
                  pl.BlockSpec((ROW_BLK, D_MODEL), lambda i: (i, 0)),
                  pl.BlockSpec(memory_space=pl.ANY)],
        out_specs=pl.BlockSpec(memory_space=pl.ANY),
        out_shape=jax.ShapeDtypeStruct(xs0.shape, xs0.dtype),
        scratch_shapes=[pltpu.SemaphoreType.DMA(())],
        input_output_aliases={2: 0},
        compiler_params=pltpu.CompilerParams(dimension_semantics=("arbitrary",),
                                             vmem_limit_bytes=VMEM_LIMIT, has_side_effects=True),
        name="moe_dispatch",
    )(dest3, h2, xs0)


def _experts_kernel(bexp_ref, nused_ref, xs_ref, w1_ref, w3_ref, w2_ref, y_ref):
    del bexp_ref
    b = pl.program_id(0)

    @pl.when(b < nused_ref[0])
    def _():
        x = xs_ref[...].astype(BF16)
        h1 = jnp.dot(x, w1_ref[...].astype(BF16), preferred_element_type=F32)
        h3 = jnp.dot(x, w3_ref[...].astype(BF16), preferred_element_type=F32)
        a = (_silu(h1) * h3).astype(BF16)
        y_ref[...] = jnp.dot(a, w2_ref[...].astype(BF16), preferred_element_type=F32)

    @pl.when(b >= nused_ref[0])
    def _():
        y_ref[...] = jnp.zeros(y_ref.shape, F32)


def _experts(block_exp, nused, xs, w1, w3, w2, layer):
    n_rows = xs.shape[0]
    d = D_MODEL
    grid_spec = pltpu.PrefetchScalarGridSpec(
        num_scalar_prefetch=2,
        grid=(n_rows // MOE_BLK,),
        in_specs=[pl.BlockSpec((MOE_BLK, d), lambda b, be, nu: (jnp.minimum(b, nu[0] - 1), 0)),
                  pl.BlockSpec((None, None, d, D_EXPERT), lambda b, be, nu: (layer, be[b], 0, 0)),
                  pl.BlockSpec((None, None, d, D_EXPERT), lambda b, be, nu: (layer, be[b], 0, 0)),
                  pl.BlockSpec((None, None, D_EXPERT, d), lambda b, be, nu: (layer, be[b], 0, 0))],
        out_specs=pl.BlockSpec((MOE_BLK, d), lambda b, be, nu: (b, 0)),
    )
    return pl.pallas_call(
        _experts_kernel,
        grid_spec=grid_spec,
        out_shape=jax.ShapeDtypeStruct((n_rows, d), F32),
        compiler_params=_cparams(1),
        name="moe_experts",
    )(block_exp, nused, xs, w1, w3, w2)


def _combine_kernel(dcur_ref, dnext_ref, x1_ref, r_ref, mod_ref, lng_ref, lnb_ref, y_ref, o_ref, ybuf, sem):
    i = pl.program_id(0)
    n_rows = x1_ref.shape[0]

    def gather(d_ref, dst_slot, r, kk):
        src = d_ref[0, 0, 2 * r + kk]
        return pltpu.make_async_copy(y_ref.at[pl.ds(src, 1)], ybuf.at[dst_slot, kk, pl.ds(r, 1)], sem.at[dst_slot])

    def start_rows(d_ref, dst_slot, r0, count):
        for rr in range(count):
            for kk in range(2):
                gather(d_ref, dst_slot, r0 + rr, kk).start(priority=kk)

    def wait_all(dst_slot):
        def body(r, c):
            for kk in range(2):
                gather(dcur_ref, dst_slot, r, kk).wait()
            return c
        lax.fori_loop(0, n_rows, body, 0, unroll=ROW_COPY_UNROLL)

    @pl.when(i == 0)
    def _():
        def body(g, c):
            start_rows(dcur_ref, 0, g * ROW_COPY_UNROLL, ROW_COPY_UNROLL)
            return c
        lax.fori_loop(0, n_rows // ROW_COPY_UNROLL, body, 0)

    g2 = mod_ref[5:6, :]
    lng = lng_ref[1:2, :]
    lnb = lnb_ref[1:2, :]

    def step(slot):
        other = 1 - slot

        start_rows(dnext_ref, other, 0, n_rows)
        wait_all(slot)
        ymix = ybuf[slot, 0] * r_ref[:, 2:3] + ybuf[slot, 1] * r_ref[:, 3:4]
        o_ref[...] = _layer_norm(DN_ALPHA * x1_ref[...] + g2 * ymix, lng, lnb)

        @pl.when(i == pl.num_programs(0) - 1)
        def _():
            wait_all(other)

    @pl.when(lax.rem(i, 2) == 0)
    def _():
        step(0)

    @pl.when(lax.rem(i, 2) == 1)
    def _():
        step(1)


def _combine(dest3, x1, r, modl, lng, lnb, y, *, ctx_blk):
    nblk = dest3.shape[0]
    d = D_MODEL
    return pl.pallas_call(
        _combine_kernel,
        grid=(nblk,),
        in_specs=[pl.BlockSpec((1, 1, 2 * ROW_BLK), lambda i: (i, 0, 0), memory_space=pltpu.SMEM),
                  pl.BlockSpec((1, 1, 2 * ROW_BLK), lambda i: (jnp.minimum(i + 1, nblk - 1), 0, 0),
                               memory_space=pltpu.SMEM),
                  pl.BlockSpec((ROW_BLK, d), lambda i: (i, 0)),
                  pl.BlockSpec((ROW_BLK, LANES), lambda i: (i, 0)),
                  _mod_spec(ctx_blk),
                  pl.BlockSpec((2, d), lambda i: (0, 0)),
                  pl.BlockSpec((2, d), lambda i: (0, 0)),
                  pl.BlockSpec(memory_space=pl.ANY)],
        out_specs=pl.BlockSpec((ROW_BLK, d), lambda i: (i, 0)),
        out_shape=jax.ShapeDtypeStruct((nblk * ROW_BLK, d), F32),
        scratch_shapes=[pltpu.VMEM((2, 2, ROW_BLK, d), F32), pltpu.SemaphoreType.DMA((2,))],
        compiler_params=_cparams(1),
        name="moe_combine",
    )(dest3, dest3, x1, r, modl, lng, lnb, y)


def _dispatch_plan(r):
    ntok = r.shape[0]
    ef = r[:, 0:2].astype(jnp.int32).reshape(-1)
    na = 2 * ntok
    sub = 256
    oh = (ef[:, None] == jnp.arange(N_EXPERTS, dtype=jnp.int32)[None, :]).astype(F32)
    ohb = oh.reshape(na // sub, sub, N_EXPERTS)
    tril = jnp.tril(jnp.ones((sub, sub), F32))
    within = jnp.einsum("ij,bjk->bik", tril, ohb)
    tot = within[:, -1, :]
    before = jnp.cumsum(tot, axis=0) - tot
    rank = jnp.sum((within + before[:, None, :]) * ohb, axis=2).reshape(na) - 1.0
    counts = jnp.sum(tot, axis=0).astype(jnp.int32)
    padded = (counts + MOE_BLK - 1) // MOE_BLK * MOE_BLK
    pad_ends = jnp.cumsum(padded)
    pad_starts = pad_ends - padded
    dest = jnp.sum(oh * pad_starts.astype(F32)[None, :], axis=1) + rank
    dest = dest.astype(jnp.int32)
    n_blocks = na // MOE_BLK + N_EXPERTS
    blk_start = jnp.arange(n_blocks, dtype=jnp.int32) * MOE_BLK
    block_exp = jnp.sum((pad_ends[None, :] <= blk_start[:, None]).astype(jnp.int32), axis=1)
    block_exp = jnp.minimum(block_exp, N_EXPERTS - 1)
    nused = (pad_ends[-1:] // MOE_BLK).astype(jnp.int32)
    return dest.reshape(ntok // ROW_BLK, 1, 2 * ROW_BLK), block_exp, nused, n_blocks


def _moe(h2, x1, r, modl, lng, lnb, w1, w3, w2, *, layer, ctx_blk):
    dest3, block_exp, nused, n_blocks = _dispatch_plan(r)
    xs0 = jnp.zeros((n_blocks * MOE_BLK, D_MODEL), F32)
    xs = _dispatch(h2, dest3, xs0)
    y = _experts(block_exp, nused, xs, w1, w3, w2, layer)
    return _combine(dest3, x1, r, modl, lng, lnb, y, ctx_blk=ctx_blk)


def _rope_tables(n):
    tkn = jnp.arange(n, dtype=jnp.int32)
    row = (tkn // GRID_W).astype(F32)
    col = (tkn % GRID_W).astype(F32)
    half = HEAD_DIM // 2
    inv = ROPE_THETA ** (-jnp.arange(0, half, 2, dtype=F32) / half)
    ang_r = row[:, None] * inv[None, :]
    ang_c = col[:, None] * inv[None, :]
    ang = jnp.concatenate([ang_r, ang_r, ang_c, ang_c] * 2, axis=-1)
    cos = jnp.concatenate([jnp.cos(ang), jnp.ones((CTX_LEN, LANES), F32)], axis=0)
    sin = jnp.concatenate([jnp.sin(ang), jnp.zeros((CTX_LEN, LANES), F32)], axis=0)
    return cos, sin


def _kv_block(t):
    for cand in (1280, 1024, 768, 512, 256):
        if t % cand == 0:
            return cand
    raise ValueError(f"unsupported token count {t}")


def kernel(x, c, ctx, c_ctx, w_mod, b_mod, ln_g, ln_b, w_in_even, w_out_even, diff_lam, diff_subln_g,
           gqa_qk_g, w_in_odd, w_out_odd, na_rpb, router_g, router_e, w1, w3, w2):
    b, n, d = x.shape
    assert b == 1 and d == D_MODEL and ctx.shape == (1, CTX_LEN, D_MODEL)
    assert n % ROW_BLK == 0 and n // GRID_W >= BAND_ROWS and CTX_LEN == ROW_BLK
    t = n + CTX_LEN
    ctx_blk = n // ROW_BLK
    xt = jnp.concatenate([x[0], ctx[0]], axis=0)
    cvec = jnp.zeros((8, d), F32).at[0].set(c[0]).at[1].set(c_ctx)
    mod = _modulation(cvec, w_mod, b_mod)[:, 0:2, :].reshape(DEPTH, 2, 6, d)
    cos, sin = _rope_tables(n)
    hm = jnp.asarray(np.kron(np.eye(2), np.full((HEAD_DIM, HEAD_DIM), 1.0 / HEAD_DIM)), F32)
    rw = jnp.zeros((DEPTH, d, LANES), F32)
    rw = rw.at[:, :, 0:N_GROUPS].set(router_g).at[:, :, N_GROUPS:N_GROUPS + N_EXPERTS].set(router_e)
    bk = _kv_block(t)
    bq = ATTN_BQ if n % ATTN_BQ == 0 else ROW_BLK
    out = None
    for l in range(DEPTH):
        last = l == DEPTH - 1
        i = l // 2
        modl = mod[l]
        if l % 2 == 0:
            lam_init = 0.8 - 0.6 * math.exp(-0.3 * l)
            q, k, v, qn, kmx = _proj_even(xt, modl, w_in_even[i].astype(BF16), cos, sin,
                                          jnp.tile(gqa_qk_g[i], (1, 2)), hm)
            subg = diff_subln_g[i].reshape(LANES, 1)
            o_lat = _attn_even(q, k, v, qn, kmx, diff_lam[i], subg, q_row0=0, n_q=n, k_row0=0, n_k=t,
                               bq=bq, bk=bk, lam_init=lam_init)
            o_ctx = None if last else _attn_even(q, k, v, qn, kmx, diff_lam[i], subg, q_row0=n, n_q=CTX_LEN,
                                                 k_row0=n, n_k=CTX_LEN, bq=CTX_LEN, bk=CTX_LEN,
                                                 lam_init=lam_init)
            w_out = w_out_even[i]
        else:
            q, k, v = _proj_odd(xt, modl, w_in_odd[i].astype(BF16))
            bm = _natten_bias_tables(na_rpb[i], n)
            o_lat = _natten(q, k, v, bm, n)
            o_ctx = None if last else _ctx_attn(q, k, v)
            w_out = w_out_odd[i]
        x1, h2, r = _post_attn(o_ctx, o_lat, xt, modl, w_out.astype(BF16), ln_g[l], ln_b[l], rw[l],
                               with_ctx=not last)
        xt = _moe(h2, x1, r, modl, ln_g[l], ln_b[l], w1, w3, w2, layer=l, ctx_blk=ctx_blk)
        out = xt
    return out.reshape(1, n, d)
```

```python
import functools
import math

import numpy as np
import jax
import jax.numpy as jnp
from jax import lax
from jax.experimental import pallas as pl
from jax.experimental.pallas import tpu as pltpu

D_MODEL = 1024
DEPTH = 4
GRID_W = 64
CTX_LEN = 256
HEAD_DIM = 64
A_HEADS = 4
B_HEADS = 8
B_KV_HEADS = 2
C_HEADS = D_MODEL // HEAD_DIM
NA_WIN_H = 8
NA_WIN_W = 16
RPB_H = 2 * NA_WIN_H - 1
RPB_W = 2 * NA_WIN_W - 1
N_GROUPS = 4
EXPERTS_PER_GROUP = 8
N_EXPERTS = N_GROUPS * EXPERTS_PER_GROUP
D_EXPERT = 512
ROPE_THETA = 10000.0
DN_ALPHA = (2.0 * DEPTH) ** 0.25
EVEN_IN = 2304
LN_EPS = 1e-6

LANES = 128
ROW_BLK = 256
MOE_BLK = 512
ROW_COPY_UNROLL = 8
N_MAPS = 16
N_KV_CHUNKS = 5
ATTN_BQ = 1024
LOG2E = math.log2(math.e)
SAFE_LOG2_GAP = 80.0
NORM_SLACK = 1.001
BAND_ROWS = 10
NA_QBLK = 2 * GRID_W
NA_STEP_BLKS = 8
NEG = -1e30
VMEM_LIMIT = 48 * 1024 * 1024

F32 = jnp.float32
BF16 = jnp.bfloat16
HIGHEST = lax.Precision.HIGHEST


def _cparams(n_axes):
    return pltpu.CompilerParams(dimension_semantics=("arbitrary",) * n_axes,
                                vmem_limit_bytes=VMEM_LIMIT)


def _dot_nt(a, b):
    return lax.dot_general(a, b, (((1,), (1,)), ((), ())), preferred_element_type=F32)


def _layer_norm(v, g, b):
    mu = jnp.mean(v, axis=-1, keepdims=True)
    d = v - mu
    var = jnp.mean(d * d, axis=-1, keepdims=True)
    return d * lax.rsqrt(var + LN_EPS) * g + b


def _silu(v):
    return v / (1.0 + jnp.exp(-v))


def _mod_kernel(c_ref, w_ref, b_ref, o_ref):
    s = _silu(c_ref[...])
    o_ref[...] = jnp.dot(s, w_ref[...], preferred_element_type=F32, precision=HIGHEST) + b_ref[...]


def _modulation(cvec, w_mod, b_mod):
    d = D_MODEL
    return pl.pallas_call(
        _mod_kernel,
        grid=(DEPTH, 6),
        in_specs=[pl.BlockSpec((8, d), lambda l, j: (0, 0)),
                  pl.BlockSpec((None, d, d), lambda l, j: (l, 0, j)),
                  pl.BlockSpec((None, 1, d), lambda l, j: (l, 0, j))],
        out_specs=pl.BlockSpec((None, 8, d), lambda l, j: (l, 0, j)),
        out_shape=jax.ShapeDtypeStruct((DEPTH, 8, 6 * d), F32),
        compiler_params=_cparams(2),
        name="modulation",
    )(cvec, w_mod, b_mod.reshape(DEPTH, 1, 6 * d))


def _mod_spec(ctx_blk):
    return pl.BlockSpec((None, 6, D_MODEL), lambda i: (jnp.where(i == ctx_blk, 1, 0), 0, 0))


def _rope(z, cos, sin, first16):
    rot = jnp.where(first16, -pltpu.roll(z, LANES - 16, 1), pltpu.roll(z, 16, 1))
    return z * cos + rot * sin


def _proj_even_kernel(x_ref, mod_ref, w_ref, cos_ref, sin_ref, g_ref, hm_ref, qt_ref, k_ref, vt_ref, qn_ref, kmx_ref):
    x = x_ref[...]
    h = (x * (1.0 + mod_ref[1:2, :]) + mod_ref[0:1, :]).astype(BF16)
    y = jnp.dot(h, w_ref[...], preferred_element_type=F32)
    cos = cos_ref[...]
    sin = sin_ref[...]
    lane = lax.broadcasted_iota(jnp.int32, cos.shape, 1)
    first16 = (lane % 32) < 16
    lo = lane < HEAD_DIM
    hm = hm_ref[...]
    scale = HEAD_DIM ** -0.5 * LOG2E

    def chunk(c):
        return y[:, c * LANES:(c + 1) * LANES]

    def qk_norm(z, g):
        ms = jnp.dot(z * z, hm, preferred_element_type=F32, precision=HIGHEST)
        return z * lax.rsqrt(ms + LN_EPS) * g

    def put_t(ref, idx, val):
        ref[idx] = val.T.astype(BF16)

    def put_q(idx, val):
        vb = val.T.astype(BF16)
        qt_ref[idx] = vb
        vf = vb.astype(F32)
        qn_ref[idx:idx + 1, :] = jnp.sqrt(jnp.sum(vf * vf, axis=0, keepdims=True))

    kmax = jnp.zeros(kmx_ref.shape, F32)
    klane = lax.broadcasted_iota(jnp.int32, kmx_ref.shape, 1)

    def put_k(idx, val, kmax):
        kb = val.astype(BF16)
        k_ref[idx] = kb
        kf = kb.astype(F32)
        n2 = jnp.max(jnp.sum(kf * kf, axis=1, keepdims=True), axis=0, keepdims=True)
        return jnp.where(klane == idx, n2, kmax)

    for hd in range(A_HEADS):
        q = _rope(chunk(hd), cos, sin, first16) * scale
        put_q(hd, jnp.where(lo, q, 0.0))
        put_q(A_HEADS + hd, jnp.where(lo, 0.0, q))
        kmax = put_k(hd, _rope(chunk(4 + hd), cos, sin, first16), kmax)
        put_t(vt_ref, hd, chunk(8 + hd))
    gq = g_ref[0:1, :]
    gk = g_ref[1:2, :]
    for pr in range(B_HEADS // 2):
        q = _rope(qk_norm(chunk(12 + pr), gq), cos, sin, first16) * scale
        qs = pltpu.roll(q, HEAD_DIM, 1)
        if pr < 2:
            put_q(8 + 2 * pr, jnp.where(lo, q, 0.0))
            put_q(8 + 2 * pr + 1, jnp.where(lo, qs, 0.0))
        else:
            put_q(8 + 2 * pr, jnp.where(lo, 0.0, qs))
            put_q(8 + 2 * pr + 1, jnp.where(lo, 0.0, q))
    kmax = put_k(4, _rope(qk_norm(chunk(16), gk), cos, sin, first16), kmax)
    put_t(vt_ref, 4, chunk(17))
    kmx_ref[...] = kmax


def _proj_even(xt, modl, w_in, cos, sin, g2, hm):
    t = xt.shape[0]
    nb = t // ROW_BLK
    return pl.pallas_call(
        _proj_even_kernel,
        grid=(nb,),
        in_specs=[pl.BlockSpec((ROW_BLK, D_MODEL), lambda i: (i, 0)),
                  _mod_spec(nb - 1),
                  pl.BlockSpec((D_MODEL, EVEN_IN), lambda i: (0, 0)),
                  pl.BlockSpec((ROW_BLK, LANES), lambda i: (i, 0)),
                  pl.BlockSpec((ROW_BLK, LANES), lambda i: (i, 0)),
                  pl.BlockSpec((2, LANES), lambda i: (0, 0)),
                  pl.BlockSpec((LANES, LANES), lambda i: (0, 0))],
        out_specs=[pl.BlockSpec((N_MAPS, LANES, ROW_BLK), lambda i: (0, 0, i)),
                   pl.BlockSpec((N_KV_CHUNKS, ROW_BLK, LANES), lambda i: (0, i, 0)),
                   pl.BlockSpec((N_KV_CHUNKS, LANES, ROW_BLK), lambda i: (0, 0, i)),
                   pl.BlockSpec((N_MAPS, ROW_BLK), lambda i: (0, i)),
                   pl.BlockSpec((None, 8, LANES), lambda i: (i, 0, 0))],
        out_shape=[jax.ShapeDtypeStruct((N_MAPS, LANES, t), BF16),
                   jax.ShapeDtypeStruct((N_KV_CHUNKS, t, LANES), BF16),
                   jax.ShapeDtypeStruct((N_KV_CHUNKS, LANES, t), BF16),
                   jax.ShapeDtypeStruct((N_MAPS, t), F32),
                   jax.ShapeDtypeStruct((nb, 8, LANES), F32)],
        compiler_params=_cparams(1),
        name="proj_even",
    )(xt, modl, w_in, cos, sin, g2, hm)


def _kv_chunk(mm):
    return mm % A_HEADS if mm < 2 * A_HEADS else A_HEADS


def _attn_even_kernel(kmax_ref, lam_ref, subg_ref, qn_ref, qt_ref, k_ref, vt_ref, o_ref, m_sc, l_sc, acc_sc, mg_sc,
                      *, nkv, lam_init):
    j = pl.program_id(1)

    @pl.when(j == 0)
    def _():
        m_sc[...] = jnp.full(m_sc.shape, NEG, F32)
        mg_sc[...] = jnp.full(mg_sc.shape, NEG, F32)
        l_sc[...] = jnp.zeros(l_sc.shape, F32)
        acc_sc[...] = jnp.zeros(acc_sc.shape, F32)

    sub = lax.broadcasted_iota(jnp.int32, (N_MAPS, 1), 0)
    chunk_of_map = jnp.where(sub >= 2 * A_HEADS, A_HEADS, sub & (A_HEADS - 1))
    kvec = jnp.zeros((N_MAPS, 1), F32)
    for c in range(N_KV_CHUNKS):
        kvec = jnp.where(chunk_of_map == c, kmax_ref[j, c] * NORM_SLACK, kvec)
    keep_ref = jnp.max(qn_ref[...] * kvec - mg_sc[...]) < SAFE_LOG2_GAP

    def scores(mm):
        return jnp.dot(k_ref[_kv_chunk(mm)], qt_ref[mm], preferred_element_type=F32)

    def accumulate_keep(mm, s):
        p = jnp.exp2(s - m_sc[mm])
        l_sc[mm] = l_sc[mm] + jnp.sum(p, axis=0, keepdims=True)
        acc_sc[mm] = acc_sc[mm] + jnp.dot(vt_ref[_kv_chunk(mm)], p.astype(BF16), preferred_element_type=F32)

    def accumulate_move(mm, s):
        m_prev = m_sc[mm]
        m_new = jnp.maximum(m_prev, jnp.max(s, axis=0, keepdims=True))
        alpha = jnp.exp2(m_prev - m_new)
        p = jnp.exp2(s - m_new)
        l_sc[mm] = alpha * l_sc[mm] + jnp.sum(p, axis=0, keepdims=True)
        pv = jnp.dot(vt_ref[_kv_chunk(mm)], p.astype(BF16), preferred_element_type=F32)
        acc_sc[mm] = alpha * acc_sc[mm] + pv
        m_sc[mm] = m_new
        mg_sc[mm:mm + 1, :] = m_new

    def all_maps(accumulate):
        s_next = scores(0)
        for mm in range(N_MAPS):
            s_cur = s_next
            if mm + 1 < N_MAPS:
                s_next = scores(mm + 1)
            accumulate(mm, s_cur)

    @pl.when(keep_ref)
    def _():
        all_maps(accumulate_keep)

    @pl.when(jnp.logical_not(keep_ref))
    def _():
        all_maps(accumulate_move)

    @pl.when(j == nkv - 1)
    def _():
        lamv = lam_ref[...]
        e1 = jnp.exp(jnp.sum(lamv[0:1, :] * lamv[1:2, :], axis=1, keepdims=True))
        e2 = jnp.exp(jnp.sum(lamv[2:3, :] * lamv[3:4, :], axis=1, keepdims=True))
        lam = e1 - e2 + lam_init
        subg = subg_ref[...]
        for hd in range(A_HEADS):
            o = acc_sc[hd] / l_sc[hd] - lam * (acc_sc[A_HEADS + hd] / l_sc[A_HEADS + hd])
            ms = jnp.mean(o * o, axis=0, keepdims=True)
            o = o * lax.rsqrt(ms + LN_EPS) * subg * (1.0 - lam_init)
            o_ref[:, hd * LANES:(hd + 1) * LANES] = o.T.astype(BF16)
        for pr in range(B_HEADS // 2):
            oa = acc_sc[8 + 2 * pr] / l_sc[8 + 2 * pr]
            ob = acc_sc[8 + 2 * pr + 1] / l_sc[8 + 2 * pr + 1]
            r0 = 0 if pr < 2 else HEAD_DIM
            o = jnp.concatenate([oa[r0:r0 + HEAD_DIM, :], ob[r0:r0 + HEAD_DIM, :]], axis=0)
            o_ref[:, (A_HEADS + pr) * LANES:(A_HEADS + pr + 1) * LANES] = o.T.astype(BF16)


def _attn_even(qt, k, vt, qn, kmx, lamf, subg, *, q_row0, n_q, k_row0, n_k, bq, bk, lam_init):
    assert q_row0 % bq == 0 and n_q % bq == 0 and k_row0 % bk == 0 and n_k % bk == 0 and bk % ROW_BLK == 0
    nq = n_q // bq
    nkv = n_k // bk
    qoff = q_row0 // bq
    koff = k_row0 // bk
    kmax = kmx[k_row0 // ROW_BLK:(k_row0 + n_k) // ROW_BLK, 0, 0:8]
    kmax = jnp.sqrt(jnp.max(kmax.reshape(nkv, bk // ROW_BLK, 8), axis=1))
    kern = functools.partial(_attn_even_kernel, nkv=nkv, lam_init=lam_init)
    return pl.pallas_call(
        kern,
        grid=(nq, nkv),
        in_specs=[pl.BlockSpec(memory_space=pltpu.SMEM),
                  pl.BlockSpec((4, HEAD_DIM), lambda i, j: (0, 0)),
                  pl.BlockSpec((LANES, 1), lambda i, j: (0, 0)),
                  pl.BlockSpec((N_MAPS, bq), lambda i, j: (0, i + qoff)),
                  pl.BlockSpec((N_MAPS, LANES, bq), lambda i, j: (0, 0, i + qoff)),
                  pl.BlockSpec((N_KV_CHUNKS, bk, LANES), lambda i, j: (0, j + koff, 0)),
                  pl.BlockSpec((N_KV_CHUNKS, LANES, bk), lambda i, j: (0, 0, j + koff))],
        out_specs=pl.BlockSpec((bq, D_MODEL), lambda i, j: (i, 0)),
        out_shape=jax.ShapeDtypeStruct((n_q, D_MODEL), BF16),
        scratch_shapes=[pltpu.VMEM((N_MAPS, 1, bq), F32), pltpu.VMEM((N_MAPS, 1, bq), F32),
                        pltpu.VMEM((N_MAPS, LANES, bq), F32), pltpu.VMEM((N_MAPS, bq), F32)],
        compiler_params=_cparams(2),
        name="attn_even",
    )(kmax, lamf, subg, qn, qt, k, vt)


def _proj_odd_kernel(x_ref, mod_ref, w_ref, q_ref, k_ref, v_ref):
    x = x_ref[...]
    h = (x * (1.0 + mod_ref[1:2, :]) + mod_ref[0:1, :]).astype(BF16)
    d = D_MODEL
    q_ref[...] = (jnp.dot(h, w_ref[:, 0:d], preferred_element_type=F32) * (HEAD_DIM ** -0.5)).astype(BF16)
    k_ref[...] = jnp.dot(h, w_ref[:, d:2 * d], preferred_element_type=F32).astype(BF16)
    v_ref[...] = jnp.dot(h, w_ref[:, 2 * d:3 * d], preferred_element_type=F32).astype(BF16)


def _proj_odd(xt, modl, w_in):
    t = xt.shape[0]
    d = D_MODEL
    spec = pl.BlockSpec((ROW_BLK, d), lambda i: (i, 0))
    return pl.pallas_call(
        _proj_odd_kernel,
        grid=(t // ROW_BLK,),
        in_specs=[spec, _mod_spec(t // ROW_BLK - 1), pl.BlockSpec((d, 3 * d), lambda i: (0, 0))],
        out_specs=[spec, spec, spec],
        out_shape=[jax.ShapeDtypeStruct((t, d), BF16)] * 3,
        compiler_params=_cparams(1),
        name="proj_odd",
    )(xt, modl, w_in)


def _softmax_av(parts):
    m = None
    for s, _ in parts:
        sm = jnp.max(s, axis=1, keepdims=True)
        m = sm if m is None else jnp.maximum(m, sm)
    l = None
    o = None
    for s, v in parts:
        p = jnp.exp(s - m)
        ps = jnp.sum(p, axis=1, keepdims=True)
        pv = jnp.dot(p.astype(BF16), v, preferred_element_type=F32)
        l = ps if l is None else l + ps
        o = pv if o is None else o + pv
    return o / l


def _natten_kernel(q_ref, k_ref, v_ref, bm_ref, o_ref, *, rows, nqb):
    n = rows * GRID_W
    nband = BAND_ROWS * GRID_W
    kc = k_ref[n:n + CTX_LEN, :]
    vc = v_ref[n:n + CTX_LEN, :]
    lane = lax.broadcasted_iota(jnp.int32, (NA_QBLK, LANES), 1)
    lo = lane < HEAD_DIM
    def scores(sub):
        b = pl.program_id(1) * NA_STEP_BLKS + sub
        band = jnp.clip(2 * b - NA_WIN_H // 2, 0, rows - BAND_ROWS)
        typ = jnp.where(b < 2, b, jnp.where(b >= nqb - 2, b - (nqb - 5), 2))
        start = pl.multiple_of(band * GRID_W, GRID_W)
        kb = k_ref[pl.ds(start, nband), :]
        vb = v_ref[pl.ds(start, nband), :]
        q = q_ref[sub * NA_QBLK:(sub + 1) * NA_QBLK, :]
        zero = jnp.zeros_like(q)
        qs = jnp.concatenate([jnp.where(lo, q, zero), jnp.where(lo, zero, q)], axis=0)
        return _dot_nt(qs, kb), _dot_nt(qs, kc), typ, vb

    nxt = scores(0)
    for sub in range(NA_STEP_BLKS):
        s_nb, s_cx, typ, vb = nxt
        if sub + 1 < NA_STEP_BLKS:
            nxt = scores(sub + 1)
        bias = bm_ref[typ].reshape(2 * NA_QBLK, nband)
        o = _softmax_av([(s_nb + bias, vb), (s_cx, vc)])
        o_ref[sub * NA_QBLK:(sub + 1) * NA_QBLK, :] = jnp.where(lo, o[:NA_QBLK], o[NA_QBLK:]).astype(BF16)


def _natten(q, k, v, bm, n):
    t = q.shape[0]
    rows = n // GRID_W
    nqb = n // NA_QBLK
    assert nqb % NA_STEP_BLKS == 0 and nqb >= 5 and rows % 2 == 0
    npair = C_HEADS // 2
    step_rows = NA_STEP_BLKS * NA_QBLK
    kern = functools.partial(_natten_kernel, rows=rows, nqb=nqb)
    return pl.pallas_call(
        kern,
        grid=(npair, nqb // NA_STEP_BLKS),
        in_specs=[pl.BlockSpec((step_rows, LANES), lambda j, b: (b, j)),
                  pl.BlockSpec((t, LANES), lambda j, b: (0, j)),
                  pl.BlockSpec((t, LANES), lambda j, b: (0, j)),
                  pl.BlockSpec((5, 2, NA_QBLK, BAND_ROWS * GRID_W), lambda j, b: (0, j, 0, 0))],
        out_specs=pl.BlockSpec((step_rows, LANES), lambda j, b: (b, j)),
        out_shape=jax.ShapeDtypeStruct((n, D_MODEL), BF16),
        compiler_params=_cparams(2),
        name="natten",
    )(q, k, v, bm)


def _ctx_attn_kernel(q_ref, k_ref, v_ref, o_ref):
    q = q_ref[...]
    k = k_ref[...]
    v = v_ref[...]
    lane = lax.broadcasted_iota(jnp.int32, q.shape, 1)
    lo = lane < HEAD_DIM
    zero = jnp.zeros_like(q)
    outs = []
    for half in range(2):
        qh = jnp.where(lo, q, zero) if half == 0 else jnp.where(lo, zero, q)
        outs.append(_softmax_av([(_dot_nt(qh, k), v)]))
    o_ref[...] = jnp.where(lo, outs[0], outs[1]).astype(BF16)


def _ctx_attn(q, k, v):
    ctx_blk = q.shape[0] // CTX_LEN - 1
    spec = pl.BlockSpec((CTX_LEN, LANES), lambda j: (ctx_blk, j))
    return pl.pallas_call(
        _ctx_attn_kernel,
        grid=(C_HEADS // 2,),
        in_specs=[spec, spec, spec],
        out_specs=pl.BlockSpec((CTX_LEN, LANES), lambda j: (0, j)),
        out_shape=jax.ShapeDtypeStruct((CTX_LEN, D_MODEL), BF16),
        compiler_params=_cparams(1),
        name="ctx_attn",
    )(q, k, v)


def _natten_bias_tables(rpb, n):
    rows = n // GRID_W
    nb = rows // 2
    c = np.arange(GRID_W)
    sel = (np.arange(RPB_W)[:, None, None] == (c[None, None, :] - c[None, :, None] + NA_WIN_W - 1))
    toep = jnp.einsum("hdm,mck->hdck", rpb, jnp.asarray(sel, F32), precision=HIGHEST)
    cs = np.clip(c - NA_WIN_W // 2, 0, GRID_W - NA_WIN_W)
    kc = np.arange(GRID_W)
    col_ok = (kc[None, :] >= cs[:, None]) & (kc[None, :] < cs[:, None] + NA_WIN_W)
    colmask = jnp.asarray(np.where(col_ok, 0.0, NEG), F32)
    negblk = jnp.full((C_HEADS, GRID_W, GRID_W), NEG, F32)
    tables = []
    for b in (0, 1, 2, nb - 2, nb - 1):
        band = int(np.clip(2 * b - NA_WIN_H // 2, 0, rows - BAND_ROWS))
        qrows = []
        for qr in range(2):
            r = 2 * b + qr
            rs = int(np.clip(r - NA_WIN_H // 2, 0, rows - NA_WIN_H))
            blks = []
            for i in range(BAND_ROWS):
                kr = band + i
                if rs <= kr < rs + NA_WIN_H:
                    blks.append(toep[:, kr - r + NA_WIN_H - 1] + colmask[None])
                else:
                    blks.append(negblk)
            qrows.append(jnp.concatenate(blks, axis=2))
        tables.append(jnp.concatenate(qrows, axis=1))
    return jnp.stack(tables, axis=0)


def _post_attn_kernel(*refs, with_ctx, ctx_blk):
    if with_ctx:
        octx_ref, refs = refs[0], refs[1:]
    olat_ref, x_ref, mod_ref, wout_ref, lng_ref, lnb_ref, rwh_ref, rwl_ref, x1_ref, h2_ref, r_ref = refs
    o = olat_ref[...]
    if with_ctx:
        o = jnp.where(pl.program_id(0) == ctx_blk, octx_ref[...], o)
    mix = jnp.dot(o, wout_ref[...], preferred_element_type=F32)
    x1 = _layer_norm(DN_ALPHA * x_ref[...] + mod_ref[2:3, :] * mix, lng_ref[0:1, :], lnb_ref[0:1, :])
    x1_ref[...] = x1
    h2 = x1 * (1.0 + mod_ref[4:5, :]) + mod_ref[3:4, :]
    h2_ref[...] = h2
    h_hi = h2.astype(BF16)
    h_lo = (h2 - h_hi.astype(F32)).astype(BF16)
    logits = (jnp.dot(h_hi, rwh_ref[...], preferred_element_type=F32)
              + jnp.dot(h_hi, rwl_ref[...], preferred_element_type=F32)
              + jnp.dot(h_lo, rwh_ref[...], preferred_element_type=F32))
    lane = lax.broadcasted_iota(jnp.int32, logits.shape, 1).astype(F32)
    big = float(LANES)
    lg = jnp.where(lane < N_GROUPS, logits, NEG)
    mg = jnp.max(lg, axis=1, keepdims=True)
    gstar = jnp.min(jnp.where(lg == mg, lane, big), axis=1, keepdims=True)
    p_top = 1.0 / jnp.sum(jnp.exp(lg - mg), axis=1, keepdims=True)
    e_lo = N_GROUPS + EXPERTS_PER_GROUP * gstar
    le = jnp.where((lane >= e_lo) & (lane < e_lo + EXPERTS_PER_GROUP), logits, NEG)
    v1 = jnp.max(le, axis=1, keepdims=True)
    i1 = jnp.min(jnp.where(le == v1, lane, big), axis=1, keepdims=True)
    le2 = jnp.where(lane == i1, NEG, le)
    v2 = jnp.max(le2, axis=1, keepdims=True)
    i2 = jnp.min(jnp.where(le2 == v2, lane, big), axis=1, keepdims=True)
    e2 = jnp.exp(v2 - v1)
    w1 = p_top / (1.0 + e2)
    w2 = p_top * e2 / (1.0 + e2)
    r = jnp.where(lane == 0.0, i1 - N_GROUPS, 0.0)
    r = jnp.where(lane == 1.0, i2 - N_GROUPS, r)
    r = jnp.where(lane == 2.0, w1, r)
    r = jnp.where(lane == 3.0, w2, r)
    r_ref[...] = r


def _post_attn(o_ctx, o_lat, xt, modl, w_out, lng, lnb, rw, *, with_ctx):
    t = xt.shape[0]
    d = D_MODEL
    ctx_blk = (t - CTX_LEN) // ROW_BLK
    nblk = ctx_blk + 1 if with_ctx else ctx_blk
    row = pl.BlockSpec((ROW_BLK, d), lambda i: (i, 0))
    in_specs = [pl.BlockSpec((ROW_BLK, d), lambda i: (jnp.minimum(i, ctx_blk - 1), 0)),
                row, _mod_spec(ctx_blk),
                pl.BlockSpec((d, d), lambda i: (0, 0)),
                pl.BlockSpec((2, d), lambda i: (0, 0)),
                pl.BlockSpec((2, d), lambda i: (0, 0)),
                pl.BlockSpec((d, LANES), lambda i: (0, 0)),
                pl.BlockSpec((d, LANES), lambda i: (0, 0))]
    rw_hi = rw.astype(BF16)
    rw_lo = (rw - rw_hi.astype(F32)).astype(BF16)
    args = [o_lat, xt, modl, w_out, lng, lnb, rw_hi, rw_lo]
    if with_ctx:
        in_specs = [pl.BlockSpec((ROW_BLK, d), lambda i: (0, 0))] + in_specs
        args = [o_ctx] + args
    kern = functools.partial(_post_attn_kernel, with_ctx=with_ctx, ctx_blk=ctx_blk)
    return pl.pallas_call(
        kern,
        grid=(nblk,),
        in_specs=in_specs,
        out_specs=[row, row, pl.BlockSpec((ROW_BLK, LANES), lambda i: (i, 0))],
        out_shape=[jax.ShapeDtypeStruct((nblk * ROW_BLK, d), F32),
                   jax.ShapeDtypeStruct((nblk * ROW_BLK, d), F32),
                   jax.ShapeDtypeStruct((nblk * ROW_BLK, LANES), F32)],
        compiler_params=_cparams(1),
        name="post_attn",
    )(*args)


def _row_copies(n_rows, make):
    def start(r, c):
        for kk in range(2):
            make(r, kk).start(priority=kk)
        return c

    def wait(r, c):
        for kk in range(2):
            make(r, kk).wait()
        return c

    for r in range(n_rows):
        start(r, 0)
    lax.fori_loop(0, n_rows, wait, 0, unroll=ROW_COPY_UNROLL)


def _dispatch_kernel(dest_ref, h_ref, xs_in_ref, xs_ref, sem):
    del xs_in_ref

    def make(r, kk):
        dst = dest_ref[0, 0, 2 * r + kk]
        return pltpu.make_async_copy(h_ref.at[pl.ds(r, 1)], xs_ref.at[pl.ds(dst, 1)], sem)

    _row_copies(h_ref.shape[0], make)


def _dispatch(h2, dest3, xs0):
    nblk = dest3.shape[0]
    return pl.pallas_call(
        _dispatch_kernel,
        grid=(nblk,),
        in_specs=[pl.BlockSpec((1, 1, 2 * ROW_BLK), lambda i: (i, 0, 0), memory_space=pltpu.SMEM),
                  pl.BlockSpec((ROW_BLK, D_MODEL), lambda i: (i, 0)),
                  pl.BlockSpec(memory_space=pl.ANY)],
        out_specs=pl.BlockSpec(memory_space=pl.ANY),
        out_shape=jax.ShapeDtypeStruct(xs0.shape, xs0.dtype),
        scratch_shapes=[pltpu.SemaphoreType.DMA(())],
        input_output_aliases={2: 0},
        compiler_params=pltpu.CompilerParams(dimension_semantics=("arbitrary",),
                                             vmem_limit_bytes=VMEM_LIMIT, has_side_effects=True),
        name="moe_dispatch",
    )(dest3, h2, xs0)


def _experts_kernel(bexp_ref, nused_ref, xs_ref, w1_ref, w3_ref, w2_ref, y_ref):
    del bexp_ref
    b = pl.program_id(0)

    @pl.when(b < nused_ref[0])
    def _():
        x = xs_ref[...].astype(BF16)
        h1 = jnp.dot(x, w1_ref[...].astype(BF16), preferred_element_type=F32)
        h3 = jnp.dot(x, w3_ref[...].astype(BF16), preferred_element_type=F32)
        a = (_silu(h1) * h3).astype(BF16)
        y_ref[...] = jnp.dot(a, w2_ref[...].astype(BF16), preferred_element_type=F32)

    @pl.when(b >= nused_ref[0])
    def _():
        y_ref[...] = jnp.zeros(y_ref.shape, F32)


def _experts(block_exp, nused, xs, w1, w3, w2, layer):
    n_rows = xs.shape[0]
    d = D_MODEL
    grid_spec = pltpu.PrefetchScalarGridSpec(
        num_scalar_prefetch=2,
        grid=(n_rows // MOE_BLK,),
        in_specs=[pl.BlockSpec((MOE_BLK, d), lambda b, be, nu: (jnp.minimum(b, nu[0] - 1), 0)),
                  pl.BlockSpec((None, None, d, D_EXPERT), lambda b, be, nu: (layer, be[b], 0, 0)),
                  pl.BlockSpec((None, None, d, D_EXPERT), lambda b, be, nu: (layer, be[b], 0, 0)),
                  pl.BlockSpec((None, None, D_EXPERT, d), lambda b, be, nu: (layer, be[b], 0, 0))],
        out_specs=pl.BlockSpec((MOE_BLK, d), lambda b, be, nu: (b, 0)),
    )
    return pl.pallas_call(
        _experts_kernel,
        grid_spec=grid_spec,
        out_shape=jax.ShapeDtypeStruct((n_rows, d), F32),
        compiler_params=_cparams(1),
        name="moe_experts",
    )(block_exp, nused, xs, w1, w3, w2)


def _combine_kernel(dcur_ref, dnext_ref, x1_ref, r_ref, mod_ref, lng_ref, lnb_ref, y_ref, o_ref, ybuf, sem):
    i = pl.program_id(0)
    n_rows = x1_ref.shape[0]

    def gather(d_ref, dst_slot, r, kk):
        src = d_ref[0, 0, 2 * r + kk]
        return pltpu.make_async_copy(y_ref.at[pl.ds(src, 1)], ybuf.at[dst_slot, kk, pl.ds(r, 1)], sem.at[dst_slot])

    def start_rows(d_ref, dst_slot, r0, count):
        for rr in range(count):
            for kk in range(2):
                gather(d_ref, dst_slot, r0 + rr, kk).start(priority=kk)

    def wait_all(dst_slot):
        def body(r, c):
            for kk in range(2):
                gather(dcur_ref, dst_slot, r, kk).wait()
            return c
        lax.fori_loop(0, n_rows, body, 0, unroll=ROW_COPY_UNROLL)

    @pl.when(i == 0)
    def _():
        def body(g, c):
            start_rows(dcur_ref, 0, g * ROW_COPY_UNROLL, ROW_COPY_UNROLL)
            return c
        lax.fori_loop(0, n_rows // ROW_COPY_UNROLL, body, 0)

    g2 = mod_ref[5:6, :]
    lng = lng_ref[1:2, :]
    lnb = lnb_ref[1:2, :]

    def step(slot):
        other = 1 - slot

        start_rows(dnext_ref, other, 0, n_rows)
        wait_all(slot)
        ymix = ybuf[slot, 0] * r_ref[:, 2:3] + ybuf[slot, 1] * r_ref[:, 3:4]
        o_ref[...] = _layer_norm(DN_ALPHA * x1_ref[...] + g2 * ymix, lng, lnb)

        @pl.when(i == pl.num_programs(0) - 1)
        def _():
            wait_all(other)

    @pl.when(lax.rem(i, 2) == 0)
    def _():
        step(0)

    @pl.when(lax.rem(i, 2) == 1)
    def _():
        step(1)


def _combine(dest3, x1, r, modl, lng, lnb, y, *, ctx_blk):
    nblk = dest3.shape[0]
    d = D_MODEL
    return pl.pallas_call(
        _combine_kernel,
        grid=(nblk,),
        in_specs=[pl.BlockSpec((1, 1, 2 * ROW_BLK), lambda i: (i, 0, 0), memory_space=pltpu.SMEM),
                  pl.BlockSpec((1, 1, 2 * ROW_BLK), lambda i: (jnp.minimum(i + 1, nblk - 1), 0, 0),
                               memory_space=pltpu.SMEM),
                  pl.BlockSpec((ROW_BLK, d), lambda i: (i, 0)),
                  pl.BlockSpec((ROW_BLK, LANES), lambda i: (i, 0)),
                  _mod_spec(ctx_blk),
                  pl.BlockSpec((2, d), lambda i: (0, 0)),
                  pl.BlockSpec((2, d), lambda i: (0, 0)),
                  pl.BlockSpec(memory_space=pl.ANY)],
        out_specs=pl.BlockSpec((ROW_BLK, d), lambda i: (i, 0)),
        out_shape=jax.ShapeDtypeStruct((nblk * ROW_BLK, d), F32),
        scratch_shapes=[pltpu.VMEM((2, 2, ROW_BLK, d), F32), pltpu.SemaphoreType.DMA((2,))],
        compiler_params=_cparams(1),
        name="moe_combine",
    )(dest3, dest3, x1, r, modl, lng, lnb, y)


def _dispatch_plan(r):
    ntok = r.shape[0]
    ef = r[:, 0:2].astype(jnp.int32).reshape(-1)
    na = 2 * ntok
    sub = 256
    oh = (ef[:, None] == jnp.arange(N_EXPERTS, dtype=jnp.int32)[None, :]).astype(F32)
    ohb = oh.reshape(na // sub, sub, N_EXPERTS)
    tril = jnp.tril(jnp.ones((sub, sub), F32))
    within = jnp.einsum("ij,bjk->bik", tril, ohb)
    tot = within[:, -1, :]
    before = jnp.cumsum(tot, axis=0) - tot
    rank = jnp.sum((within + before[:, None, :]) * ohb, axis=2).reshape(na) - 1.0
    counts = jnp.sum(tot, axis=0).astype(jnp.int32)
    padded = (counts + MOE_BLK - 1) // MOE_BLK * MOE_BLK
    pad_ends = jnp.cumsum(padded)
    pad_starts = pad_ends - padded
    dest = jnp.sum(oh * pad_starts.astype(F32)[None, :], axis=1) + rank
    dest = dest.astype(jnp.int32)
    n_blocks = na // MOE_BLK + N_EXPERTS
    blk_start = jnp.arange(n_blocks, dtype=jnp.int32) * MOE_BLK
    block_exp = jnp.sum((pad_ends[None, :] <= blk_start[:, None]).astype(jnp.int32), axis=1)
    block_exp = jnp.minimum(block_exp, N_EXPERTS - 1)
    nused = (pad_ends[-1:] // MOE_BLK).astype(jnp.int32)
    return dest.reshape(ntok // ROW_BLK, 1, 2 * ROW_BLK), block_exp, nused, n_blocks


def _moe(h2, x1, r, modl, lng, lnb, w1, w3, w2, *, layer, ctx_blk):
    dest3, block_exp, nused, n_blocks = _dispatch_plan(r)
    xs0 = jnp.zeros((n_blocks * MOE_BLK, D_MODEL), F32)
    xs = _dispatch(h2, dest3, xs0)
    y = _experts(block_exp, nused, xs, w1, w3, w2, layer)
    return _combine(dest3, x1, r, modl, lng, lnb, y, ctx_blk=ctx_blk)


def _rope_tables(n):
    tkn = jnp.arange(n, dtype=jnp.int32)
    row = (tkn // GRID_W).astype(F32)
    col = (tkn % GRID_W).astype(F32)
    half = HEAD_DIM // 2
    inv = ROPE_THETA ** (-jnp.arange(0, half, 2, dtype=F32) / half)
    ang_r = row[:, None] * inv[None, :]
    ang_c = col[:, None] * inv[None, :]
    ang = jnp.concatenate([ang_r, ang_r, ang_c, ang_c] * 2, axis=-1)
    cos = jnp.concatenate([jnp.cos(ang), jnp.ones((CTX_LEN, LANES), F32)], axis=0)
    sin = jnp.concatenate([jnp.sin(ang), jnp.zeros((CTX_LEN, LANES), F32)], axis=0)
    return cos, sin


def _kv_block(t):
    for cand in (1280, 1024, 768, 512, 256):
        if t % cand == 0:
            return cand
    raise ValueError(f"unsupported token count {t}")


def kernel(x, c, ctx, c_ctx, w_mod, b_mod, ln_g, ln_b, w_in_even, w_out_even, diff_lam, diff_subln_g,
           gqa_qk_g, w_in_odd, w_out_odd, na_rpb, router_g, router_e, w1, w3, w2):
    b, n, d = x.shape
    assert b == 1 and d == D_MODEL and ctx.shape == (1, CTX_LEN, D_MODEL)
    assert n % ROW_BLK == 0 and n // GRID_W >= BAND_ROWS and CTX_LEN == ROW_BLK
    t = n + CTX_LEN
    ctx_blk = n // ROW_BLK
    xt = jnp.concatenate([x[0], ctx[0]], axis=0)
    cvec = jnp.zeros((8, d), F32).at[0].set(c[0]).at[1].set(c_ctx)
    mod = _modulation(cvec, w_mod, b_mod)[:, 0:2, :].reshape(DEPTH, 2, 6, d)
    cos, sin = _rope_tables(n)
    hm = jnp.asarray(np.kron(np.eye(2), np.full((HEAD_DIM, HEAD_DIM), 1.0 / HEAD_DIM)), F32)
    rw = jnp.zeros((DEPTH, d, LANES), F32)
    rw = rw.at[:, :, 0:N_GROUPS].set(router_g).at[:, :, N_GROUPS:N_GROUPS + N_EXPERTS].set(router_e)
    bk = _kv_block(t)
    bq = ATTN_BQ if n % ATTN_BQ == 0 else ROW_BLK
    out = None
    for l in range(DEPTH):
        last = l == DEPTH - 1
        i = l // 2
        modl = mod[l]
        if l % 2 == 0:
            lam_init = 0.8 - 0.6 * math.exp(-0.3 * l)
            q, k, v, qn, kmx = _proj_even(xt, modl, w_in_even[i].astype(BF16), cos, sin,
                                          jnp.tile(gqa_qk_g[i], (1, 2)), hm)
            subg = diff_subln_g[i].reshape(LANES, 1)
            o_lat = _attn_even(q, k, v, qn, kmx, diff_lam[i], subg, q_row0=0, n_q=n, k_row0=0, n_k=t,
                               bq=bq, bk=bk, lam_init=lam_init)
            o_ctx = None if last else _attn_even(q, k, v, qn, kmx, diff_lam[i], subg, q_row0=n, n_q=CTX_LEN,
                                                 k_row0=n, n_k=CTX_LEN, bq=CTX_LEN, bk=CTX_LEN,
                                                 lam_init=lam_init)
            w_out = w_out_even[i]
        else:
            q, k, v = _proj_odd(xt, modl, w_in_odd[i].astype(BF16))
            bm = _natten_bias_tables(na_rpb[i], n)
            o_lat = _natten(q, k, v, bm, n)
            o_ctx = None if last else _ctx_attn(q, k, v)
            w_out = w_out_odd[i]
        x1, h2, r = _post_attn(o_ctx, o_lat, xt, modl, w_out.astype(BF16), ln_g[l], ln_b[l], rw[l],
                               with_ctx=not last)
        xt = _moe(h2, x1, r, modl, ln_g[l], ln_b[l], w1, w3, w2, layer=l, ctx_blk=ctx_blk)
        out = xt
    return out.reshape(1, n, d)
```
